```python
import jax, jax.numpy as jnp
from jax import lax
import numpy as np

D_MODEL = 1024
BATCH = 8
SEQ = 2048
DEPTH = 4
DEC_BATCH = 128
DEC_SEQ = 4
PAST_LEN = 2048
PAGE_SIZE = 128

N_A_LAYERS = DEPTH // 2
N_B_LAYERS = DEPTH - N_A_LAYERS
HEAD_DIM = 64
RWKV_HEADS = D_MODEL // HEAD_DIM
DECAY_LORA = 64
AAA_LORA = 64
MV_LORA = 32
GN_EPS = 64e-5
RMS_EPS = 1e-6
ATT_HEADS = D_MODEL // HEAD_DIM
ATT_WIDTH = ATT_HEADS * HEAD_DIM
WINDOWS = (128, 512, 2048)
DILATIONS = (1, 4, 16)
N_GROUPS = len(WINDOWS)
ATT_SCALE = HEAD_DIM ** -0.5
ROPE_THETA = 10000.0
NEG_INF = -1e30

kernel_name = 'yoco_rwkv7_dilated_swa_step'


def rms_norm(x, g):
    xf = x.astype(jnp.float32)
    y = xf * lax.rsqrt(jnp.mean(xf * xf, axis=-1, keepdims=True) + RMS_EPS)
    return (y * g.astype(jnp.float32)).astype(x.dtype)


def rope(x, pos):
    half = HEAD_DIM // 2
    inv = ROPE_THETA ** (-jnp.arange(half, dtype=jnp.float32) / half)
    ang = pos.astype(jnp.float32)[:, None] * inv[None, :]
    cos = jnp.cos(ang)[:, None, :]
    sin = jnp.sin(ang)[:, None, :]
    xf = x.astype(jnp.float32)
    x1, x2 = xf[..., :half], xf[..., half:]
    return jnp.concatenate([x1 * cos - x2 * sin, x2 * cos + x1 * sin], axis=-1).astype(x.dtype)


def wkv_step(state, inp):
    r, w, k, v, kk, b = inp
    sa = jnp.einsum('bhij,bhj->bhi', state, kk)
    state = state * w[:, :, None, :] - sa[..., None] * b[:, :, None, :] + v[..., None] * k[:, :, None, :]
    return state, jnp.einsum('bhij,bhj->bhi', state, r)


def rwkv7_layer(x, shift_prev, wkv_prev, v_first, ln_g, mu, w_in, w0, w_la, w_lb, a0, a_la, a_lb,
                k_k, k_a, r_k, gn_w, gn_b, w_out, v_res):
    B, T, D = x.shape
    H, C = RWKV_HEADS, HEAD_DIM
    f32 = jnp.float32
    xn = rms_norm(x, ln_g)
    prev = jnp.concatenate([shift_prev[:, None].astype(xn.dtype), xn[:, :-1]], axis=1)
    mixed = xn[None] + mu[:, None, None, :] * (prev - xn)[None]
    rkvz = jnp.einsum('pbtd,pde->pbte', mixed[:4], w_in)
    r = rkvz[0].astype(f32)
    k = rkvz[1].astype(f32)
    v = rkvz[2].astype(f32)
    z = rkvz[3]
    xv = mixed[2].astype(f32)
    xw = mixed[4].astype(f32)
    xa = mixed[5].astype(f32)
    w_log = -jax.nn.softplus(-(w0 + jnp.tanh(xw @ w_la) @ w_lb)) - 0.5
    decay = jnp.exp(-jnp.exp(w_log))
    if v_res is not None:
        v0, v_la, v_lb = v_res
        v = v + (v_first - v) * jax.nn.sigmoid(v0 + (xv @ v_la) @ v_lb)
    a = jax.nn.sigmoid(a0 + (xa @ a_la) @ a_lb)
    kk = (k * k_k).reshape(B, T, H, C)
    kk = kk * lax.rsqrt(jnp.maximum(jnp.sum(kk * kk, axis=-1, keepdims=True), 1e-24))
    k = k * (1.0 + (a - 1.0) * k_a)
    rh = r.reshape(B, T, H, C)
    wh = decay.reshape(B, T, H, C)
    kh = k.reshape(B, T, H, C)
    vh = v.reshape(B, T, H, C)
    ah = a.reshape(B, T, H, C)
    seq = tuple(jnp.swapaxes(t, 0, 1) for t in (rh, wh, kh, vh, kk, kk * ah))
    wkv, ys = lax.scan(wkv_step, wkv_prev.astype(f32), seq)
    y = jnp.swapaxes(ys, 0, 1)
    y_mean = jnp.mean(y, axis=-1, keepdims=True)
    y_var = jnp.mean(jnp.square(y - y_mean), axis=-1, keepdims=True)
    y = ((y - y_mean) * lax.rsqrt(y_var + GN_EPS)).reshape(B, T, D) * gn_w + gn_b
    y = y + (jnp.sum(rh * kh * r_k, axis=-1, keepdims=True) * vh).reshape(B, T, D)
    out = (y.astype(x.dtype) * jax.nn.silu(z)) @ w_out
    return x + out, xn[:, -1], wkv, v


def rwkv7_stack(x, shift_init, wkv_init, ln, mu, w_in, w0, w_la, w_lb, a0, a_la, a_lb,
                v0, v_la, v_lb, k_k, k_a, r_k, gn_w, gn_b, w_out):
    shifts, states = [], []
    v_first = None
    for l in range(N_A_LAYERS):
        v_res = None if l == 0 else (v0[l - 1], v_la[l - 1], v_lb[l - 1])
        x, sh, st, v = rwkv7_layer(x, shift_init[l], wkv_init[l], v_first, ln[l], mu[l], w_in[l], w0[l],
                                   w_la[l], w_lb[l], a0[l], a_la[l], a_lb[l], k_k[l], k_a[l], r_k[l],
                                   gn_w[l], gn_b[l], w_out[l], v_res)
        if l == 0:
            v_first = v
        shifts.append(sh)
        states.append(st)
    return x, jnp.stack(shifts), jnp.stack(states)


def shared_kv(h, pos, ln_g, w_kv):
    B, T, _ = h.shape
    kv = (rms_norm(h, ln_g) @ w_kv).reshape(B, T, 2, N_GROUPS * ATT_HEADS, HEAD_DIM)
    k = rope(kv[:, :, 0], pos).reshape(B, T, N_GROUPS, ATT_HEADS, HEAD_DIM)
    v = kv[:, :, 1].reshape(B, T, N_GROUPS, ATT_HEADS, HEAD_DIM)
    return k, v


def dilated_band_attention(q, k, v, window, dilation):
    B, S, H, C = q.shape
    n = window // dilation
    L = S // dilation
    nb = -(-L // n)
    Lp = nb * n

    def strided(t):
        t = jnp.swapaxes(t.reshape(B, L, dilation, H, C), 1, 2)
        t = jnp.pad(t, ((0, 0), (0, 0), (0, Lp - L), (0, 0), (0, 0)))
        return t.reshape(B, dilation, nb, n, H, C)

    def with_prev(t):
        prev = jnp.pad(t, ((0, 0), (0, 0), (1, 0), (0, 0), (0, 0), (0, 0)))[:, :, :nb]
        return jnp.concatenate([prev, t], axis=3)

    qs = strided(q)
    kb = with_prev(strided(k))
    vb = with_prev(strided(v))
    s = jnp.einsum('bdnqhc,bdnkhc->bdnhqk', qs, kb, preferred_element_type=jnp.float32) * ATT_SCALE
    qi = jnp.arange(n)[:, None]
    ki = jnp.arange(2 * n)[None, :]
    rel = n + qi - ki
    blk = jnp.arange(nb)[:, None, None]
    valid = (rel >= 0) & (rel < n) & (blk * n + ki >= n)
    s = jnp.where(valid[None, None, :, None], s, NEG_INF)
    m = jnp.max(s, axis=-1, keepdims=True)
    e = jnp.exp(s - m)
    den = jnp.sum(e, axis=-1)
    o = jnp.einsum('bdnhqk,bdnkhc->bdnqhc', e, vb.astype(jnp.float32)) / jnp.swapaxes(den, 3, 4)[..., None]
    lse = jnp.swapaxes(m[..., 0] + jnp.log(den), 3, 4)

    def unstrided(t):
        t = t.reshape((B, dilation, Lp) + t.shape[4:])[:, :, :L]
        return jnp.swapaxes(t, 1, 2).reshape((B, S) + t.shape[3:])

    return unstrided(o), unstrided(lse)


def gather_strided(k_all, v_all, n_past, n_new, window, dilation):
    n = window // dilation
    idx = n_past + jnp.arange(n_new)[:, None] - dilation * jnp.arange(n)[None, :]
    valid = idx >= 0
    idx = jnp.maximum(idx, 0)
    return k_all[:, idx], v_all[:, idx], valid


def dilated_gathered_attention(q, kg, vg, valid):
    s = jnp.einsum('bthc,btkhc->bthk', q, kg, preferred_element_type=jnp.float32) * ATT_SCALE
    s = jnp.where(valid[None, :, None, :], s, NEG_INF)
    m = jnp.max(s, axis=-1, keepdims=True)
    e = jnp.exp(s - m)
    den = jnp.sum(e, axis=-1)
    o = jnp.einsum('bthk,btkhc->bthc', e, vg.astype(jnp.float32)) / den[..., None]
    return o, m[..., 0] + jnp.log(den)


def dilated_layer(x, pos, ln_g, w_in, w_out, attend):
    B, T, _ = x.shape
    xn = rms_norm(x, ln_g)
    proj = xn @ w_in
    q_all = proj[..., :N_GROUPS * ATT_WIDTH].reshape(B, T, N_GROUPS * ATT_HEADS, HEAD_DIM)
    q_all = rope(q_all, pos).reshape(B, T, N_GROUPS, ATT_HEADS, HEAD_DIM)
    z = proj[..., N_GROUPS * ATT_WIDTH:]
    outs, lses = [], []
    for g in range(N_GROUPS):
        o, l = attend(g, q_all[:, :, g])
        outs.append(o)
        lses.append(l)
    alpha = jax.nn.softmax(jnp.stack(lses), axis=0)
    o = jnp.einsum('gbth,gbthc->bthc', alpha, jnp.stack(outs)).reshape(B, T, ATT_WIDTH)
    return x + (o.astype(x.dtype) * jax.nn.silu(z)) @ w_out


def setup_inputs(seed: int = 0) -> dict:
    key = jax.random.key(seed)
    keys = iter(jax.random.split(key, 40))
    f32 = jnp.float32
    D, H, C = D_MODEL, RWKV_HEADS, HEAD_DIM
    NA, NB, G = N_A_LAYERS, N_B_LAYERS, N_GROUPS

    def normal(shape, scale):
        return scale * jax.random.normal(next(keys), shape, f32)

    def gain(shape):
        return 1.0 + 0.02 * jax.random.normal(next(keys), shape, f32)

    inp = {}
    inp['x_prompt'] = normal((BATCH, SEQ, D), 1.0)
    inp['x_sample'] = normal((DEC_BATCH, DEC_SEQ, D), 1.0)
    inp['state_wkv'] = normal((NA, DEC_BATCH, H, C, C), 0.3)
    inp['state_shift'] = normal((NA, DEC_BATCH, D), 1.0)
    inp['cache_kv_g0'] = normal((DEC_BATCH, min(WINDOWS[0], PAST_LEN), 2, ATT_HEADS, C), 1.0)
    inp['cache_kv_g1'] = normal((DEC_BATCH, min(WINDOWS[1], PAST_LEN), 2, ATT_HEADS, C), 1.0)
    inp['cache_kv_g2'] = normal((DEC_BATCH, min(WINDOWS[2], PAST_LEN), 2, ATT_HEADS, C), 1.0)
    inp['a_ln'] = gain((NA, D))
    inp['a_mu'] = jax.random.uniform(next(keys), (NA, 6, D), f32)
    inp['a_w_in'] = normal((NA, 4, D, D), D ** -0.5)
    inp['a_w0'] = jax.random.uniform(next(keys), (NA, D), f32, -3.0, 0.0)
    inp['a_w_lora_a'] = normal((NA, D, DECAY_LORA), D ** -0.5)
    inp['a_w_lora_b'] = normal((NA, DECAY_LORA, D), 0.1 * DECAY_LORA ** -0.5)
    inp['a_a0'] = normal((NA, D), 0.1)
    inp['a_a_lora_a'] = normal((NA, D, AAA_LORA), D ** -0.5)
    inp['a_a_lora_b'] = normal((NA, AAA_LORA, D), 0.1 * AAA_LORA ** -0.5)
    inp['a_v0'] = normal((NA - 1, D), 0.1)
    inp['a_v_lora_a'] = normal((NA - 1, D, MV_LORA), D ** -0.5)
    inp['a_v_lora_b'] = normal((NA - 1, MV_LORA, D), 0.1 * MV_LORA ** -0.5)
    inp['a_k_k'] = 0.85 + 0.02 * jax.random.normal(next(keys), (NA, D), f32)
    inp['a_k_a'] = gain((NA, D))
    inp['a_r_k'] = normal((NA, H, C), 0.1)
    inp['a_gn_w'] = gain((NA, D))
    inp['a_gn_b'] = normal((NA, D), 0.02)
    inp['a_w_out'] = normal((NA, D, D), D ** -0.5)
    inp['kv_ln'] = gain((D,))
    inp['w_kv'] = normal((D, 2 * G * ATT_WIDTH), D ** -0.5)
    inp['b_ln'] = gain((NB, D))
    inp['b_w_in'] = normal((NB, D, (G + 1) * ATT_WIDTH), D ** -0.5)
    inp['b_w_out'] = normal((NB, ATT_WIDTH, D), ATT_WIDTH ** -0.5)
    inp['final_ln'] = gain((D,))
    return inp


def reference(x_prompt, x_sample, state_wkv, state_shift, cache_kv_g0, cache_kv_g1, cache_kv_g2,
              a_ln, a_mu, a_w_in, a_w0, a_w_lora_a, a_w_lora_b, a_a0, a_a_lora_a, a_a_lora_b,
              a_v0, a_v_lora_a, a_v_lora_b, a_k_k, a_k_a, a_r_k, a_gn_w, a_gn_b, a_w_out,
              kv_ln, w_kv, b_ln, b_w_in, b_w_out, final_ln):
    a_weights = (a_ln, a_mu, a_w_in, a_w0, a_w_lora_a, a_w_lora_b, a_a0, a_a_lora_a, a_a_lora_b,
                 a_v0, a_v_lora_a, a_v_lora_b, a_k_k, a_k_a, a_r_k, a_gn_w, a_gn_b, a_w_out)
    Bp, S, _ = x_prompt.shape
    T = x_sample.shape[1]
    pos_p = jnp.arange(S, dtype=jnp.int32)
    pos_s = PAST_LEN + jnp.arange(T, dtype=jnp.int32)

    zero_shift = jnp.zeros((N_A_LAYERS, Bp, D_MODEL), x_prompt.dtype)
    zero_wkv = jnp.zeros((N_A_LAYERS, Bp, RWKV_HEADS, HEAD_DIM, HEAD_DIM), jnp.float32)
    hp, shift_p, wkv_p = rwkv7_stack(x_prompt, zero_shift, zero_wkv, *a_weights)
    hs, shift_s, wkv_s = rwkv7_stack(x_sample, state_shift, state_wkv, *a_weights)

    kp, vp = shared_kv(hp, pos_p, kv_ln, w_kv)
    ks, vs = shared_kv(hs, pos_s, kv_ln, w_kv)

    caches = (cache_kv_g0, cache_kv_g1, cache_kv_g2)
    gathered = []
    for g in range(N_GROUPS):
        cache = caches[g]
        k_all = jnp.concatenate([cache[:, :, 0].astype(ks.dtype), ks[:, :, g]], axis=1)
        v_all = jnp.concatenate([cache[:, :, 1].astype(vs.dtype), vs[:, :, g]], axis=1)
        gathered.append(gather_strided(k_all, v_all, cache.shape[1], T, WINDOWS[g], DILATIONS[g]))

    def attend_prompt(g, q):
        return dilated_band_attention(q, kp[:, :, g], vp[:, :, g], WINDOWS[g], DILATIONS[g])

    def attend_sample(g, q):
        kg, vg, valid = gathered[g]
        return dilated_gathered_attention(q, kg, vg, valid)

    for l in range(N_B_LAYERS):
        hp = dilated_layer(hp, pos_p, b_ln[l], b_w_in[l], b_w_out[l], attend_prompt)
        hs = dilated_layer(hs, pos_s, b_ln[l], b_w_in[l], b_w_out[l], attend_sample)

    y_prompt = rms_norm(hp, final_ln)
    y_sample = rms_norm(hs, final_ln)

    kv_p = []
    kv_s = []
    for g in range(N_GROUPS):
        wc = min(WINDOWS[g], S)
        kv_p.append(jnp.stack([kp[:, S - wc:, g], vp[:, S - wc:, g]], axis=2))
        kv_s.append(jnp.stack([ks[:, :, g], vs[:, :, g]], axis=2))
    return (y_prompt, y_sample, wkv_p, wkv_s, shift_p, shift_s,
            kv_p[0], kv_s[0], kv_p[1], kv_s[1], kv_p[2], kv_s[2])
```

```python
import functools

import jax
import jax.numpy as jnp
from jax import lax
from jax.experimental import pallas as pl
from jax.experimental.pallas import tpu as pltpu

F32 = jnp.float32
BF16 = jnp.bfloat16

D_MODEL = 1024
HEAD_DIM = 64
N_HEADS = D_MODEL // HEAD_DIM
LANES = 128
N_PAIRS = D_MODEL // LANES
N_GROUPS = 3
WINDOWS = (128, 512, 2048)
DILATIONS = (1, 4, 16)
KEYS_PER_QUERY = 128
ATT_SCALE = HEAD_DIM ** -0.5
ROPE_THETA = 10000.0
NEG_INF = -1e30
GN_EPS = 64e-5
RMS_EPS = 1e-6
PAST_LEN = 2048
WKV_CHUNK = 64
VMEM_LIMIT = 56 * 1024 * 1024

_HI = lax.Precision.HIGHEST


def _dot(a, b):
    return jnp.dot(a, b, preferred_element_type=F32)


def _dot_nt(a, b, precision=None):
    return lax.dot_general(a, b, (((1,), (1,)), ((), ())), precision=precision, preferred_element_type=F32)


def _dot_tn(a, b, precision=None):
    return lax.dot_general(a, b, (((0,), (0,)), ((), ())), precision=precision, preferred_element_type=F32)


def _mm(a, b):
    return jnp.dot(a, b, precision=_HI, preferred_element_type=F32)


def _rms(x, g):
    return x * lax.rsqrt(jnp.mean(x * x, axis=-1, keepdims=True) + RMS_EPS) * g


def _sigmoid(x):
    return 1.0 / (1.0 + jnp.exp(-x))


def _block_ones():
    r = lax.broadcasted_iota(jnp.int32, (LANES, LANES), 0) // HEAD_DIM
    c = lax.broadcasted_iota(jnp.int32, (LANES, LANES), 1) // HEAD_DIM
    return (r == c).astype(BF16)


def _head_sum(x, bo, two_pass=True):
    outs = []
    for g in range(x.shape[1] // LANES):
        xg = x[:, g * LANES:(g + 1) * LANES]
        hi = xg.astype(BF16)
        acc = _dot(hi, bo)
        if two_pass:
            lo = (xg - hi.astype(F32)).astype(BF16)
            acc = acc + _dot(lo, bo)
        outs.append(acc)
    return outs[0] if len(outs) == 1 else jnp.concatenate(outs, axis=1)


def _a_pre_body(has_vres, per_row_shift, seq_len, *refs):
    it = iter(refs)
    x_ref, sp_ref = next(it), next(it)
    vf_ref = next(it) if has_vres else None
    ln_ref, mu_ref, win_ref = next(it), next(it), next(it)
    w0_ref, wla_ref, wlb_ref = next(it), next(it), next(it)
    a0_ref, ala_ref, alb_ref = next(it), next(it), next(it)
    if has_vres:
        v0_ref, vla_ref, vlb_ref = next(it), next(it), next(it)
    kk_ref, ka_ref = next(it), next(it)
    r_o, lw_o, k_o, v_o, kk_o, b_o, z_o, sh_o = [next(it) for _ in range(8)]
    scr = next(it)

    tm = x_ref.shape[0]
    xn = _rms(x_ref[...], ln_ref[...])
    rolled = pltpu.roll(xn, 1, axis=0)
    row = lax.broadcasted_iota(jnp.int32, (tm, 1), 0)
    if per_row_shift:
        prev = jnp.where(row % seq_len == 0, sp_ref[...], rolled)
        for g in range(N_PAIRS):
            lanes = slice(g * LANES, (g + 1) * LANES)
            scr[g] = xn[:, lanes]
            sh_o[:, lanes] = scr[g, pl.ds(seq_len - 1, tm // seq_len, stride=seq_len), :]
    else:
        first = jnp.where(pl.program_id(1) == 0, sp_ref[...], scr[0:1, :])
        prev = jnp.where(row == 0, first, rolled)
        scr[0:1, :] = xn[tm - 1:tm, :]
        sh_o[...] = xn[tm - 1:tm, :]

    dx = prev - xn
    mu = mu_ref[...]

    def mixed(p):
        return xn + mu[p:p + 1, :] * dx

    xv = mixed(2)
    r = _dot(mixed(0).astype(BF16), win_ref[0])
    k = _dot(mixed(1).astype(BF16), win_ref[1])
    v = _dot(xv.astype(BF16), win_ref[2])
    z_o[...] = _dot(mixed(3).astype(BF16), win_ref[3])

    def lora(xm, la_ref, lb_ref, act):
        hdn = _dot(xm.astype(BF16), la_ref[...])
        if act:
            hdn = jnp.tanh(hdn)
        return _dot(hdn.astype(BF16), lb_ref[...])

    wl = w0_ref[...] + lora(mixed(4), wla_ref, wlb_ref, True)
    sp = jnp.maximum(-wl, 0.0) + jnp.log1p(jnp.exp(-jnp.abs(wl)))
    lw_o[...] = -jnp.exp(-sp - 0.5)
    a = _sigmoid(a0_ref[...] + lora(mixed(5), ala_ref, alb_ref, False))
    if has_vres:
        v = v + (vf_ref[...] - v) * _sigmoid(v0_ref[...] + lora(xv, vla_ref, vlb_ref, False))

    bo = _block_ones()
    kkp = k * kk_ref[...]
    kk = kkp * lax.rsqrt(jnp.maximum(_head_sum(kkp * kkp, bo), 1e-24))
    r_o[...] = r
    k_o[...] = k * (1.0 + (a - 1.0) * ka_ref[...])
    v_o[...] = v
    kk_o[...] = kk
    b_o[...] = kk * a


def _a_pre(x, shift_prev, v_first, w, n_seq, seq_len):
    n = x.shape[0]
    has_vres = v_first is not None
    per_row_shift = seq_len < 256
    if per_row_shift:
        tm = n
        grid = (1, 1)
        sp = jnp.repeat(shift_prev, seq_len, axis=0)
        sp_spec = pl.BlockSpec((tm, D_MODEL), lambda b, i: (0, 0))
        sh_shape = jax.ShapeDtypeStruct((n_seq, D_MODEL), F32)
        sh_spec = pl.BlockSpec((n_seq, D_MODEL), lambda b, i: (0, 0))
        scr = pltpu.VMEM((N_PAIRS, tm, LANES), F32)
        tiles = 1
    else:
        tm = 256
        tiles = seq_len // tm
        grid = (n_seq, tiles)
        sp = shift_prev.reshape(n_seq, 1, D_MODEL)
        sp_spec = pl.BlockSpec((None, 1, D_MODEL), lambda b, i: (b, 0, 0))
        sh_shape = jax.ShapeDtypeStruct((n_seq, 1, D_MODEL), F32)
        sh_spec = pl.BlockSpec((None, 1, D_MODEL), lambda b, i: (b, 0, 0))
        scr = pltpu.VMEM((8, D_MODEL), F32)

    row_spec = pl.BlockSpec((tm, D_MODEL), lambda b, i: (b * tiles + i, 0))

    def const(shape):
        return pl.BlockSpec(shape, lambda b, i: (0,) * len(shape))

    vec = const((1, D_MODEL))
    args = [x, sp]
    specs = [row_spec, sp_spec]
    if has_vres:
        args.append(v_first)
        specs.append(row_spec)
    args += [w['ln'], w['mu'], w['w_in'], w['w0'], w['w_la'], w['w_lb'], w['a0'], w['a_la'], w['a_lb']]
    specs += [vec, const((6, D_MODEL)), const((4, D_MODEL, D_MODEL)), vec, const(w['w_la'].shape),
              const(w['w_lb'].shape), vec, const(w['a_la'].shape), const(w['a_lb'].shape)]
    if has_vres:
        args += [w['v0'], w['v_la'], w['v_lb']]
        specs += [vec, const(w['v_la'].shape), const(w['v_lb'].shape)]
    args += [w['k_k'], w['k_a']]
    specs += [vec, vec]

    full = jax.ShapeDtypeStruct((n, D_MODEL), F32)
    outs = pl.pallas_call(
        functools.partial(_a_pre_body, has_vres, per_row_shift, seq_len),
        out_shape=[full] * 7 + [sh_shape],
        grid=grid,
        in_specs=specs,
        out_specs=[row_spec] * 7 + [sh_spec],
        scratch_shapes=[scr],
        compiler_params=pltpu.CompilerParams(dimension_semantics=("arbitrary", "arbitrary"),
                                             vmem_limit_bytes=VMEM_LIMIT),
        name="a_pre",
    )(*args)
    return outs[:7], outs[7].reshape(n_seq, D_MODEL)


def _wkv_body(zero_init, *refs):
    it = iter(refs)
    r_ref, lw_ref, k_ref, v_ref, kk_ref, b_ref = [next(it) for _ in range(6)]
    st_ref = None if zero_init else next(it)
    y_ref, so_ref = next(it), next(it)
    sbd = next(it)

    L = WKV_CHUNK
    c = pl.program_id(1)
    nc = pl.num_programs(1)
    half = HEAD_DIM

    @pl.when(c == 0)
    def _():
        if zero_init:
            sbd[...] = jnp.zeros(sbd.shape, F32)
        else:
            zeros = jnp.zeros((half, half), F32)
            for p in range(N_PAIRS):
                top = jnp.concatenate([st_ref[2 * p], zeros], axis=1)
                bot = jnp.concatenate([zeros, st_ref[2 * p + 1]], axis=1)
                sbd[p] = jnp.concatenate([top, bot], axis=0)

    rr = lax.broadcasted_iota(jnp.int32, (L, L), 0)
    cc = lax.broadcasted_iota(jnp.int32, (L, L), 1)
    lw = lw_ref[...]
    g_incl = _mm((rr >= cc).astype(F32), lw)
    e_g = jnp.exp(g_incl)
    e_gx = jnp.exp(g_incl - lw)
    e_ng = jnp.exp(-g_incl)
    e_gl = e_g[L - 1:L, :]
    kt = kk_ref[...] * e_gx
    rt = r_ref[...] * e_g
    bt = b_ref[...] * e_ng
    kl = k_ref[...] * e_ng
    bh = bt * e_gl
    kh = kl * e_gl
    vv = v_ref[...]

    lane = lax.broadcasted_iota(jnp.int32, (1, LANES), 1)
    h0 = lane < half
    r2 = lax.broadcasted_iota(jnp.int32, (2 * L, 2 * L), 0)
    c2 = lax.broadcasted_iota(jnp.int32, (2 * L, 2 * L), 1)
    same_head = (r2 // L) == (c2 // L)
    strict_bd = same_head & (r2 > c2)
    diag16 = (r2 // 16) == (c2 // 16)
    eye = (r2 == c2).astype(F32)
    rl = lax.broadcasted_iota(jnp.int32, (L, 2 * L), 0)
    cl = lax.broadcasted_iota(jnp.int32, (L, 2 * L), 1) % L
    strict2 = rl > cl
    incl2 = rl >= cl

    def split(x):
        return jnp.concatenate([jnp.where(h0, x, 0.0), jnp.where(h0, 0.0, x)], axis=0)

    for p in range(N_PAIRS):
        sl = slice(p * LANES, (p + 1) * LANES)
        ktp, rtp, vp = kt[:, sl], rt[:, sl], vv[:, sl]
        lhs = jnp.concatenate([ktp, rtp], axis=0)
        rhs = jnp.concatenate([split(bt[:, sl]), split(kl[:, sl])], axis=0)
        aa = _dot_nt(lhs, rhs, _HI)
        a_kb, a_kk = aa[0:L, 0:2 * L], aa[0:L, 2 * L:4 * L]
        a_rb, a_rk = aa[L:2 * L, 0:2 * L], aa[L:2 * L, 2 * L:4 * L]

        abd = jnp.where(strict_bd, jnp.concatenate([a_kb, a_kb], axis=0), 0.0)
        a_d = jnp.where(diag16, abd, 0.0)
        a_o = abd - a_d
        a2 = _mm(a_d, a_d)
        a4 = _mm(a2, a2)
        a8 = _mm(a4, a4)
        x_d = _mm(_mm(_mm(eye - a_d, eye + a2), eye + a4), eye + a8)
        nn = _mm(x_d, a_o)
        x_f = _mm(_mm(eye - nn, eye + _mm(nn, nn)), x_d)

        s_p = sbd[p]
        pp = _dot_nt(lhs, s_p, _HI)
        v2 = split(vp)
        rhs_u = -(pp[0:L] + _mm(jnp.where(strict2, a_kk, 0.0), v2))
        ubd = _mm(x_f, split(rhs_u))
        u = ubd[0:L] + ubd[L:2 * L]
        coef = jnp.concatenate([jnp.where(incl2, a_rb, 0.0), jnp.where(incl2, a_rk, 0.0)], axis=1)
        y_ref[:, sl] = pp[L:2 * L] + _mm(coef, jnp.concatenate([split(u), v2], axis=0))
        upd = _dot_tn(jnp.concatenate([u, vp], axis=0),
                      jnp.concatenate([bh[:, sl], kh[:, sl]], axis=0), _HI)
        sbd[p] = s_p * e_gl[:, sl] + jnp.where(same_head, upd, 0.0)

    @pl.when(c == nc - 1)
    def _():
        for p in range(N_PAIRS):
            s_p = sbd[p]
            so_ref[2 * p] = s_p[0:half, 0:half]
            so_ref[2 * p + 1] = pltpu.roll(s_p[half:2 * half, :], half, axis=1)[:, 0:half]


def _wkv(seqs, state0):
    bsz, t, _ = seqs[0].shape
    zero_init = state0 is None
    tok = pl.BlockSpec((None, WKV_CHUNK, D_MODEL), lambda b, c: (b, c, 0))
    st = pl.BlockSpec((None, N_HEADS, HEAD_DIM, HEAD_DIM), lambda b, c: (b, 0, 0, 0))
    args = list(seqs) + ([] if zero_init else [state0])
    specs = [tok] * 6 + ([] if zero_init else [st])
    return pl.pallas_call(
        functools.partial(_wkv_body, zero_init),
        out_shape=[jax.ShapeDtypeStruct((bsz, t, D_MODEL), F32),
                   jax.ShapeDtypeStruct((bsz, N_HEADS, HEAD_DIM, HEAD_DIM), F32)],
        grid=(bsz, t // WKV_CHUNK),
        in_specs=specs,
        out_specs=[tok, st],
        scratch_shapes=[pltpu.VMEM((N_PAIRS, LANES, LANES), F32)],
        compiler_params=pltpu.CompilerParams(dimension_semantics=("arbitrary", "arbitrary"),
                                             vmem_limit_bytes=VMEM_LIMIT),
        name="wkv",
    )(*args)


def _a_post_body(y_ref, r_ref, k_ref, v_ref, z_ref, x_ref, rk_ref, gw_ref, gb_ref, wo_ref, o_ref):
    bo = _block_ones()
    y = y_ref[...]
    inv_c = 1.0 / HEAD_DIM
    yc = y - _head_sum(y, bo) * inv_c
    var = _head_sum(yc * yc, bo) * inv_c
    yn = yc * lax.rsqrt(var + GN_EPS) * gw_ref[...] + gb_ref[...]
    yn = yn + _head_sum(r_ref[...] * k_ref[...] * rk_ref[...], bo) * v_ref[...]
    z = z_ref[...]
    gated = yn * (z * _sigmoid(z))
    o_ref[...] = x_ref[...] + _dot(gated.astype(BF16), wo_ref[...])


def _a_post(y, r, k, v, z, x, w):
    n = x.shape[0]
    tm = 512
    row = pl.BlockSpec((tm, D_MODEL), lambda i: (i, 0))
    vec = pl.BlockSpec((1, D_MODEL), lambda i: (0, 0))
    return pl.pallas_call(
        _a_post_body,
        out_shape=jax.ShapeDtypeStruct((n, D_MODEL), F32),
        grid=(n // tm,),
        in_specs=[row] * 6 + [vec] * 3 + [pl.BlockSpec((D_MODEL, D_MODEL), lambda i: (0, 0))],
        out_specs=row,
        compiler_params=pltpu.CompilerParams(dimension_semantics=("arbitrary",), vmem_limit_bytes=VMEM_LIMIT),
        name="a_post",
    )(y, r, k, v, z, x, w['r_k'], w['gn_w'], w['gn_b'], w['w_out'])


def _rope_tile(t, cos_t, sin_t):
    lane = lax.broadcasted_iota(jnp.int32, (1, LANES), 1)
    first_half = (lane % HEAD_DIM) < (HEAD_DIM // 2)
    outs = []
    for g in range(t.shape[1] // LANES):
        xg = t[:, g * LANES:(g + 1) * LANES]
        partner = jnp.where(first_half, pltpu.roll(xg, LANES - HEAD_DIM // 2, axis=1),
                            pltpu.roll(xg, HEAD_DIM // 2, axis=1))
        outs.append(xg * cos_t + partner * sin_t)
    return jnp.concatenate(outs, axis=1)


def _proj_body(n_rope, x_ref, ln_ref, w_ref, cos_ref, sin_ref, o_ref, xn_scr):
    j = pl.program_id(1)

    @pl.when(j == 0)
    def _():
        xn_scr[...] = _rms(x_ref[...], ln_ref[...]).astype(BF16)

    t = _dot(xn_scr[...], w_ref[...])

    @pl.when(j < n_rope)
    def _():
        o_ref[...] = _rope_tile(t, cos_ref[...], sin_ref[...])

    @pl.when(j >= n_rope)
    def _():
        o_ref[...] = t


def _proj(x, ln, w, cos_t, sin_t, n_rope):
    n = x.shape[0]
    n_out = w.shape[1]
    tm = min(512, n)
    tn = D_MODEL
    n_tab = cos_t.shape[0] // tm
    tab = pl.BlockSpec((tm, LANES), lambda i, j: (i % n_tab, 0))
    return pl.pallas_call(
        functools.partial(_proj_body, n_rope),
        out_shape=jax.ShapeDtypeStruct((n, n_out), F32),
        grid=(n // tm, n_out // tn),
        in_specs=[pl.BlockSpec((tm, D_MODEL), lambda i, j: (i, 0)),
                  pl.BlockSpec((1, D_MODEL), lambda i, j: (0, 0)),
                  pl.BlockSpec((D_MODEL, tn), lambda i, j: (0, j)),
                  tab, tab],
        out_specs=pl.BlockSpec((tm, tn), lambda i, j: (i, j)),
        scratch_shapes=[pltpu.VMEM((tm, D_MODEL), BF16)],
        compiler_params=pltpu.CompilerParams(dimension_semantics=("arbitrary", "arbitrary"),
                                             vmem_limit_bytes=VMEM_LIMIT),
        name="proj",
    )(x, ln, w, cos_t, sin_t)


def _attn_prompt_body(seq_len, q0, q1, q2, k0, k1, k2, v0, v1, v2, o_ref, m_s, l_s, a_s):
    nq = KEYS_PER_QUERY
    lane = lax.broadcasted_iota(jnp.int32, (1, LANES), 1)
    h0 = lane < HEAD_DIM
    qi = lax.broadcasted_iota(jnp.int32, (nq, nq), 0)
    ki = lax.broadcasted_iota(jnp.int32, (nq, nq), 1)
    cur_mask = ki <= qi
    prev_mask = ki > qi

    def block(g, q_ref, k_ref, v_ref, d, c, nb, use_prev):
        rows = pl.ds(c + d * nq * nb, nq, stride=d)
        q = q_ref[rows, :] * ATT_SCALE
        kb = k_ref[rows, :].astype(BF16)
        vb = v_ref[rows, :].astype(BF16)
        mask = cur_mask
        if use_prev:
            prow = pl.ds(c + d * nq * (nb - 1), nq, stride=d)
            kb = jnp.concatenate([k_ref[prow, :].astype(BF16), kb], axis=0)
            vb = jnp.concatenate([v_ref[prow, :].astype(BF16), vb], axis=0)
            mask = jnp.concatenate([prev_mask, cur_mask], axis=1)
        ms, ls, accs = [], [], []
        for head_lanes in (h0, jnp.logical_not(h0)):
            qh = jnp.where(head_lanes, q, 0.0).astype(BF16)
            s = jnp.where(mask, _dot_nt(qh, kb), NEG_INF)
            mx = jnp.max(s, axis=-1, keepdims=True)
            pr = jnp.exp(s - mx)
            ms.append(mx)
            ls.append(jnp.sum(pr, axis=-1, keepdims=True))
            accs.append(_dot(pr.astype(BF16), vb))
        m_b = jnp.where(h0, ms[0], ms[1])
        l_b = jnp.where(h0, ls[0], ls[1])
        a_b = jnp.where(h0, accs[0], accs[1])
        if g > 0:
            m_o, l_o, a_o = m_s[rows, :], l_s[rows, :], a_s[rows, :]
            m_n = jnp.maximum(m_o, m_b)
            w_o = jnp.exp(m_o - m_n)
            w_b = jnp.exp(m_b - m_n)
            m_b = m_n
            l_b = w_o * l_o + w_b * l_b
            a_b = w_o * a_o + w_b * a_b
        if g == N_GROUPS - 1:
            o_ref[rows, :] = a_b / l_b
        else:
            m_s[rows, :] = m_b
            l_s[rows, :] = l_b
            a_s[rows, :] = a_b

    for g, (q_ref, k_ref, v_ref) in enumerate(((q0, k0, v0), (q1, k1, v1), (q2, k2, v2))):
        d = DILATIONS[g]
        n_blk = seq_len // (d * nq)

        def per_class(c, carry, g=g, q_ref=q_ref, k_ref=k_ref, v_ref=v_ref, d=d, n_blk=n_blk):
            block(g, q_ref, k_ref, v_ref, d, c, 0, False)
            if n_blk > 1:
                def per_block(nb, carry2):
                    block(g, q_ref, k_ref, v_ref, d, c, nb, True)
                    return carry2
                lax.fori_loop(1, n_blk, per_block, 0)
            return carry

        lax.fori_loop(0, d, per_class, 0)


def _attn_prompt(proj, kv, bsz, seq_len):
    proj3 = proj.reshape(bsz, seq_len, proj.shape[1])
    kv3 = kv.reshape(bsz, seq_len, kv.shape[1])

    def col(base):
        return pl.BlockSpec((None, seq_len, LANES), lambda b, p: (b, 0, base + p))

    q_specs = [col(g * N_PAIRS) for g in range(N_GROUPS)]
    k_specs = [col(g * N_PAIRS) for g in range(N_GROUPS)]
    v_specs = [col((N_GROUPS + g) * N_PAIRS) for g in range(N_GROUPS)]
    out = pl.pallas_call(
        functools.partial(_attn_prompt_body, seq_len),
        out_shape=jax.ShapeDtypeStruct((bsz, seq_len, D_MODEL), F32),
        grid=(bsz, N_PAIRS),
        in_specs=q_specs + k_specs + v_specs,
        out_specs=pl.BlockSpec((None, seq_len, LANES), lambda b, p: (b, 0, p)),
        scratch_shapes=[pltpu.VMEM((seq_len, LANES), F32)] * 3,
        compiler_params=pltpu.CompilerParams(dimension_semantics=("arbitrary", "arbitrary"),
                                             vmem_limit_bytes=VMEM_LIMIT),
        name="attn_prompt",
    )(proj3, proj3, proj3, kv3, kv3, kv3, kv3, kv3, kv3)
    return out.reshape(bsz * seq_len, D_MODEL)


def _attn_sample_body(n_new, q_ref, kvn_ref, c0_ref, c1_ref, c2_ref, o_ref):
    bo = _block_ones()
    kv_w = 2 * D_MODEL
    rowid = lax.broadcasted_iota(jnp.int32, (KEYS_PER_QUERY, 1), 0)
    newid = lax.broadcasted_iota(jnp.int32, (n_new, 1), 0)
    caches = (c0_ref, c1_ref, c2_ref)
    for j in range(n_new):
        m_run = l_run = a_run = None
        for g in range(N_GROUPS):
            d = DILATIONS[g]
            qj = q_ref[j:j + 1, g * D_MODEL:(g + 1) * D_MODEL] * ATT_SCALE
            k_new = kvn_ref[:, g * D_MODEL:(g + 1) * D_MODEL]
            v_new = kvn_ref[:, (N_GROUPS + g) * D_MODEL:(N_GROUPS + g + 1) * D_MODEL]
            res = j % d
            first_row = j // d + 1
            k_c = caches[g][:, res * kv_w:res * kv_w + D_MODEL]
            v_c = caches[g][:, res * kv_w + D_MODEL:(res + 1) * kv_w]
            new_ok = (newid <= j) & ((j - newid) % d == 0)
            s_c = jnp.where(rowid >= first_row, _head_sum(k_c * qj, bo, two_pass=False), NEG_INF)
            s_n = jnp.where(new_ok, _head_sum(k_new * qj, bo, two_pass=False), NEG_INF)
            m_g = jnp.maximum(jnp.max(s_c, axis=0, keepdims=True), jnp.max(s_n, axis=0, keepdims=True))
            e_c = jnp.exp(s_c - m_g)
            e_n = jnp.exp(s_n - m_g)
            l_g = jnp.sum(e_c, axis=0, keepdims=True) + jnp.sum(e_n, axis=0, keepdims=True)
            a_g = jnp.sum(e_c * v_c, axis=0, keepdims=True) + jnp.sum(e_n * v_new, axis=0, keepdims=True)
            if g == 0:
                m_run, l_run, a_run = m_g, l_g, a_g
            else:
                m_n = jnp.maximum(m_run, m_g)
                w_o = jnp.exp(m_run - m_n)
                w_g = jnp.exp(m_g - m_n)
                m_run = m_n
                l_run = w_o * l_run + w_g * l_g
                a_run = w_o * a_run + w_g * a_g
        o_ref[j:j + 1, :] = a_run / l_run


def _attn_sample(proj, kv, caches, bsz, n_new):
    proj3 = proj.reshape(bsz, n_new, proj.shape[1])
    kv3 = kv.reshape(bsz, n_new, kv.shape[1])
    views, specs = [], []
    for g in range(N_GROUPS):
        d = DILATIONS[g]
        assert caches[g].shape[1] == WINDOWS[g] and (d == 1 or n_new <= d)
        views.append(caches[g].reshape(bsz, KEYS_PER_QUERY, d * 2 * D_MODEL))
        width = min(d, n_new) * 2 * D_MODEL
        specs.append(pl.BlockSpec((None, KEYS_PER_QUERY, width), lambda b: (b, 0, 0)))
    out = pl.pallas_call(
        functools.partial(_attn_sample_body, n_new),
        out_shape=jax.ShapeDtypeStruct((bsz, n_new, D_MODEL), F32),
        grid=(bsz,),
        in_specs=[pl.BlockSpec((None, n_new, proj.shape[1]), lambda b: (b, 0, 0)),
                  pl.BlockSpec((None, n_new, kv.shape[1]), lambda b: (b, 0, 0))] + specs,
        out_specs=pl.BlockSpec((None, n_new, D_MODEL), lambda b: (b, 0, 0)),
        compiler_params=pltpu.CompilerParams(dimension_semantics=("arbitrary",), vmem_limit_bytes=VMEM_LIMIT),
        name="attn_sample",
    )(proj3, kv3, *views)
    return out.reshape(bsz * n_new, D_MODEL)


def _b_post_body(final, *refs):
    if final:
        o_ref, z_ref, x_ref, wo_ref, fl_ref, out_ref = refs
    else:
        o_ref, z_ref, x_ref, wo_ref, out_ref = refs
    z = z_ref[...]
    gated = o_ref[...] * (z * _sigmoid(z))
    h = x_ref[...] + _dot(gated.astype(BF16), wo_ref[...])
    out_ref[...] = _rms(h, fl_ref[...]) if final else h


def _b_post(o, proj, x, w_out, final_ln):
    n = x.shape[0]
    tm = min(512, n)
    final = final_ln is not None
    row = pl.BlockSpec((tm, D_MODEL), lambda i: (i, 0))
    specs = [row, pl.BlockSpec((tm, D_MODEL), lambda i: (i, N_GROUPS)), row,
             pl.BlockSpec((D_MODEL, D_MODEL), lambda i: (0, 0))]
    args = [o, proj, x, w_out]
    if final:
        specs.append(pl.BlockSpec((1, D_MODEL), lambda i: (0, 0)))
        args.append(final_ln)
    return pl.pallas_call(
        functools.partial(_b_post_body, final),
        out_shape=jax.ShapeDtypeStruct((n, D_MODEL), F32),
        grid=(n // tm,),
        in_specs=specs,
        out_specs=row,
        compiler_params=pltpu.CompilerParams(dimension_semantics=("arbitrary",), vmem_limit_bytes=VMEM_LIMIT),
        name="b_post",
    )(*args)


def _rope_tables(pos):
    half = HEAD_DIM // 2
    inv = ROPE_THETA ** (-jnp.arange(half, dtype=F32) / half)
    ang = pos.astype(F32)[:, None] * inv[None, :]
    cos, sin = jnp.cos(ang), jnp.sin(ang)
    return jnp.concatenate([cos, cos, cos, cos], axis=1), jnp.concatenate([-sin, sin, -sin, sin], axis=1)


def _rwkv_stack(x, shift_init, wkv_init, n_seq, seq_len, layers):
    shifts, states = [], []
    v_first = None
    pad = (-seq_len) % WKV_CHUNK
    for l, w in enumerate(layers):
        (r, lw, k, v, kk, b, z), sh = _a_pre(x, shift_init[l], v_first, w, n_seq, seq_len)
        if l == 0:
            v_first = v
        seqs = [t.reshape(n_seq, seq_len, D_MODEL) for t in (r, lw, k, v, kk, b)]
        if pad:
            seqs = [jnp.pad(t, ((0, 0), (0, pad), (0, 0))) for t in seqs]
        y, st = _wkv(seqs, None if wkv_init is None else wkv_init[l])
        y = y[:, :seq_len].reshape(n_seq * seq_len, D_MODEL)
        x = _a_post(y, r, k, v, z, x, w)
        shifts.append(sh)
        states.append(st)
    return x, jnp.stack(shifts), jnp.stack(states)


def kernel(x_prompt, x_sample, state_wkv, state_shift, cache_kv_g0, cache_kv_g1, cache_kv_g2, a_ln, a_mu, a_w_in, a_w0, a_w_lora_a, a_w_lora_b, a_a0, a_a_lora_a, a_a_lora_b, a_v0, a_v_lora_a, a_v_lora_b, a_k_k, a_k_a, a_r_k, a_gn_w, a_gn_b, a_w_out, kv_ln, w_kv, b_ln, b_w_in, b_w_out, final_ln):
    bp, s, _ = x_prompt.shape
    bs, t, _ = x_sample.shape
    n_a = a_ln.shape[0]
    n_b = b_ln.shape[0]

    def row(vec):
        return vec.reshape(1, D_MODEL)

    layers = []
    for l in range(n_a):
        w = dict(ln=row(a_ln[l]), mu=a_mu[l], w_in=a_w_in[l].astype(BF16), w0=row(a_w0[l]),
                 w_la=a_w_lora_a[l].astype(BF16), w_lb=a_w_lora_b[l].astype(BF16), a0=row(a_a0[l]),
                 a_la=a_a_lora_a[l].astype(BF16), a_lb=a_a_lora_b[l].astype(BF16),
                 k_k=row(a_k_k[l]), k_a=row(a_k_a[l]), r_k=row(a_r_k[l]), gn_w=row(a_gn_w[l]),
                 gn_b=row(a_gn_b[l]), w_out=a_w_out[l].astype(BF16))
        if l > 0:
            w.update(v0=row(a_v0[l - 1]), v_la=a_v_lora_a[l - 1].astype(BF16), v_lb=a_v_lora_b[l - 1].astype(BF16))
        layers.append(w)

    xp = x_prompt.reshape(bp * s, D_MODEL)
    xs = x_sample.reshape(bs * t, D_MODEL)
    zero_shift = jnp.zeros((n_a, bp, D_MODEL), F32)
    hp, shift_p, wkv_p = _rwkv_stack(xp, zero_shift, None, bp, s, layers)
    hs, shift_s, wkv_s = _rwkv_stack(xs, state_shift, state_wkv, bs, t, layers)

    cos_p, sin_p = _rope_tables(jnp.arange(s, dtype=jnp.int32))
    cos_s, sin_s = _rope_tables(jnp.tile(PAST_LEN + jnp.arange(t, dtype=jnp.int32), bs))
    w_kv_b = w_kv.astype(BF16)
    kv_ln_r = row(kv_ln)
    kvp = _proj(hp, kv_ln_r, w_kv_b, cos_p, sin_p, N_GROUPS)
    kvs = _proj(hs, kv_ln_r, w_kv_b, cos_s, sin_s, N_GROUPS)

    caches = (cache_kv_g0, cache_kv_g1, cache_kv_g2)
    for l in range(n_b):
        w_in = b_w_in[l].astype(BF16)
        w_out = b_w_out[l].astype(BF16)
        fl = row(final_ln) if l == n_b - 1 else None
        pj = _proj(hp, row(b_ln[l]), w_in, cos_p, sin_p, N_GROUPS)
        hp = _b_post(_attn_prompt(pj, kvp, bp, s), pj, hp, w_out, fl)
        pj = _proj(hs, row(b_ln[l]), w_in, cos_s, sin_s, N_GROUPS)
        hs = _b_post(_attn_sample(pj, kvs, caches, bs, t), pj, hs, w_out, fl)

    y_prompt = hp.reshape(bp, s, D_MODEL)
    y_sample = hs.reshape(bs, t, D_MODEL)

    kvp3 = kvp.reshape(bp, s, 2, N_GROUPS, N_HEADS, HEAD_DIM)
    kvs3 = kvs.reshape(bs, t, 2, N_GROUPS, N_HEADS, HEAD_DIM)
    kv_out = []
    for g in range(N_GROUPS):
        wc = min(WINDOWS[g], s)
        kv_out.append(kvp3[:, s - wc:, :, g])
        kv_out.append(kvs3[:, :, :, g])
    return (y_prompt, y_sample, wkv_p, wkv_s, shift_p, shift_s, *kv_out)
```

```python
import functools

import jax
import jax.numpy as jnp
from jax import lax
from jax.experimental import pallas as pl
from jax.experimental.pallas import tpu as pltpu

F32 = jnp.float32
BF16 = jnp.bfloat16

D_MODEL = 1024
HEAD_DIM = 64
N_HEADS = D_MODEL // HEAD_DIM
LANES = 128
N_PAIRS = D_MODEL // LANES
N_GROUPS = 3
WINDOWS = (128, 512, 2048)
DILATIONS = (1, 4, 16)
KEYS_PER_QUERY = 128
ATT_SCALE = HEAD_DIM ** -0.5
ROPE_THETA = 10000.0
NEG_INF = -1e30
GN_EPS = 64e-5
RMS_EPS = 1e-6
PAST_LEN = 2048
WKV_CHUNK = 64
WKV_SMALL_CHUNK = 8
ATTN_UNROLL = 4
VMEM_LIMIT = 56 * 1024 * 1024

_HI = lax.Precision.HIGHEST


def _dot(a, b):
    return jnp.dot(a, b, preferred_element_type=F32)


def _dot_nt(a, b):
    return lax.dot_general(a, b, (((1,), (1,)), ((), ())), preferred_element_type=F32)


def _dot_tn(a, b):
    return lax.dot_general(a, b, (((0,), (0,)), ((), ())), preferred_element_type=F32)


def _rms(x, g):
    return x * lax.rsqrt(jnp.mean(x * x, axis=-1, keepdims=True) + RMS_EPS) * g


def _sigmoid(x):
    return 1.0 / (1.0 + jnp.exp(-x))


def _block_ones():
    r = lax.broadcasted_iota(jnp.int32, (LANES, LANES), 0) // HEAD_DIM
    c = lax.broadcasted_iota(jnp.int32, (LANES, LANES), 1) // HEAD_DIM
    return (r == c).astype(BF16)


def _head_sum(x, bo):
    outs = []
    for g in range(x.shape[1] // LANES):
        xg = x[:, g * LANES:(g + 1) * LANES]
        hi = xg.astype(BF16)
        lo = (xg - hi.astype(F32)).astype(BF16)
        outs.append(_dot(hi, bo) + _dot(lo, bo))
    return outs[0] if len(outs) == 1 else jnp.concatenate(outs, axis=1)


def _a_pre_body(has_vres, per_row_shift, seq_len, *refs):
    it = iter(refs)
    x_ref, sp_ref = next(it), next(it)
    vf_ref = next(it) if has_vres else None
    ln_ref, mu_ref, win_ref = next(it), next(it), next(it)
    w0_ref, wla_ref, wlb_ref = next(it), next(it), next(it)
    a0_ref, ala_ref, alb_ref = next(it), next(it), next(it)
    if has_vres:
        v0_ref, vla_ref, vlb_ref = next(it), next(it), next(it)
    kk_ref, ka_ref = next(it), next(it)
    r_o, lw_o, k_o, v_o, kk_o, b_o, z_o, sh_o = [next(it) for _ in range(8)]
    scr = next(it)

    tm = x_ref.shape[0]
    xn = _rms(x_ref[...], ln_ref[...])
    rolled = pltpu.roll(xn, 1, axis=0)
    row = lax.broadcasted_iota(jnp.int32, (tm, 1), 0)
    if per_row_shift:
        prev = jnp.where(row % seq_len == 0, sp_ref[...], rolled)
        for g in range(N_PAIRS):
            lanes = slice(g * LANES, (g + 1) * LANES)
            scr[g] = xn[:, lanes]
            sh_o[:, lanes] = scr[g, pl.ds(seq_len - 1, tm // seq_len, stride=seq_len), :]
    else:
        first = jnp.where(pl.program_id(1) == 0, sp_ref[...], scr[0:1, :])
        prev = jnp.where(row == 0, first, rolled)
        scr[0:1, :] = xn[tm - 1:tm, :]
        sh_o[...] = xn[tm - 1:tm, :]

    dx = prev - xn
    mu = mu_ref[...]

    def mixed(p):
        return xn + mu[p:p + 1, :] * dx

    xv = mixed(2)
    r = _dot(mixed(0).astype(BF16), win_ref[0])
    k = _dot(mixed(1).astype(BF16), win_ref[1])
    v = _dot(xv.astype(BF16), win_ref[2])
    z_o[...] = _dot(mixed(3).astype(BF16), win_ref[3])

    def lora(xm, la_ref, lb_ref, act):
        hdn = _dot(xm.astype(BF16), la_ref[...])
        if act:
            hdn = jnp.tanh(hdn)
        return _dot(hdn.astype(BF16), lb_ref[...])

    wl = w0_ref[...] + lora(mixed(4), wla_ref, wlb_ref, True)
    sp = jnp.maximum(-wl, 0.0) + jnp.log1p(jnp.exp(-jnp.abs(wl)))
    lw_o[...] = -jnp.exp(-sp - 0.5)
    a = _sigmoid(a0_ref[...] + lora(mixed(5), ala_ref, alb_ref, False))
    if has_vres:
        v = v + (vf_ref[...] - v) * _sigmoid(v0_ref[...] + lora(xv, vla_ref, vlb_ref, False))

    bo = _block_ones()
    kkp = k * kk_ref[...]
    kk = kkp * lax.rsqrt(jnp.maximum(_head_sum(kkp * kkp, bo), 1e-24))
    r_o[...] = r
    k_o[...] = k * (1.0 + (a - 1.0) * ka_ref[...])
    v_o[...] = v
    kk_o[...] = kk
    b_o[...] = kk * a


def _a_pre(x, shift_prev, v_first, w, n_seq, seq_len):
    n = x.shape[0]
    has_vres = v_first is not None
    per_row_shift = seq_len < 256
    if per_row_shift:
        tm = n
        grid = (1, 1)
        sp = jnp.repeat(shift_prev, seq_len, axis=0)
        sp_spec = pl.BlockSpec((tm, D_MODEL), lambda b, i: (0, 0))
        sh_shape = jax.ShapeDtypeStruct((n_seq, D_MODEL), F32)
        sh_spec = pl.BlockSpec((n_seq, D_MODEL), lambda b, i: (0, 0))
        scr = pltpu.VMEM((N_PAIRS, tm, LANES), F32)
        tiles = 1
    else:
        tm = 256
        tiles = seq_len // tm
        grid = (n_seq, tiles)
        sp = shift_prev.reshape(n_seq, 1, D_MODEL)
        sp_spec = pl.BlockSpec((None, 1, D_MODEL), lambda b, i: (b, 0, 0))
        sh_shape = jax.ShapeDtypeStruct((n_seq, 1, D_MODEL), F32)
        sh_spec = pl.BlockSpec((None, 1, D_MODEL), lambda b, i: (b, 0, 0))
        scr = pltpu.VMEM((8, D_MODEL), F32)

    row_spec = pl.BlockSpec((tm, D_MODEL), lambda b, i: (b * tiles + i, 0))

    def const(shape):
        return pl.BlockSpec(shape, lambda b, i: (0,) * len(shape))

    vec = const((1, D_MODEL))
    args = [x, sp]
    specs = [row_spec, sp_spec]
    if has_vres:
        args.append(v_first)
        specs.append(row_spec)
    args += [w['ln'], w['mu'], w['w_in'], w['w0'], w['w_la'], w['w_lb'], w['a0'], w['a_la'], w['a_lb']]
    specs += [vec, const((6, D_MODEL)), const((4, D_MODEL, D_MODEL)), vec, const(w['w_la'].shape),
              const(w['w_lb'].shape), vec, const(w['a_la'].shape), const(w['a_lb'].shape)]
    if has_vres:
        args += [w['v0'], w['v_la'], w['v_lb']]
        specs += [vec, const(w['v_la'].shape), const(w['v_lb'].shape)]
    args += [w['k_k'], w['k_a']]
    specs += [vec, vec]

    full = jax.ShapeDtypeStruct((n, D_MODEL), F32)
    outs = pl.pallas_call(
        functools.partial(_a_pre_body, has_vres, per_row_shift, seq_len),
        out_shape=[full] * 7 + [sh_shape],
        grid=grid,
        in_specs=specs,
        out_specs=[row_spec] * 7 + [sh_spec],
        scratch_shapes=[scr],
        compiler_params=pltpu.CompilerParams(dimension_semantics=("arbitrary", "arbitrary"),
                                             vmem_limit_bytes=VMEM_LIMIT),
        name="a_pre",
    )(*args)
    return outs[:7], outs[7].reshape(n_seq, D_MODEL)


def _wkv_body(zero_init, L, *refs):
    it = iter(refs)
    r_ref, lw_ref, k_ref, v_ref, kk_ref, b_ref = [next(it) for _ in range(6)]
    st_ref = None if zero_init else next(it)
    y_ref, so_ref = next(it), next(it)
    sbd = next(it)

    n_b = r_ref.shape[0]
    c = pl.program_id(1)
    nc = pl.num_programs(1)
    half = HEAD_DIM
    chains = [(bi, p) for bi in range(n_b) for p in range(N_PAIRS)]

    @pl.when(c == 0)
    def _():
        if zero_init:
            sbd[...] = jnp.zeros(sbd.shape, F32)
        else:
            zeros = jnp.zeros((half, half), F32)
            for bi, p in chains:
                top = jnp.concatenate([st_ref[bi, 2 * p], zeros], axis=1)
                bot = jnp.concatenate([zeros, st_ref[bi, 2 * p + 1]], axis=1)
                sbd[bi, p] = jnp.concatenate([top, bot], axis=0)

    rr = lax.broadcasted_iota(jnp.int32, (L, L), 0)
    cc = lax.broadcasted_iota(jnp.int32, (L, L), 1)
    tri = (rr >= cc).astype(F32)
    lane = lax.broadcasted_iota(jnp.int32, (1, LANES), 1)
    h0 = lane < half
    r2 = lax.broadcasted_iota(jnp.int32, (2 * L, 2 * L), 0)
    c2 = lax.broadcasted_iota(jnp.int32, (2 * L, 2 * L), 1)
    strict = r2 > c2
    incl = r2 >= c2
    eye = (r2 == c2).astype(F32)
    blk = min(16, L)
    diag_blk = (r2 // blk) == (c2 // blk)

    def split(x):
        return jnp.concatenate([jnp.where(h0, x, 0.0), jnp.where(h0, 0.0, x)], axis=0).astype(BF16)

    def mm(a, b):
        return _dot(a.astype(BF16), b.astype(BF16))

    def neumann(a, n_terms):
        x = {ch: eye - a[ch] for ch in chains}
        pw, k = a, 2
        while k < n_terms:
            pw = {ch: mm(pw[ch], pw[ch]) for ch in chains}
            x = {ch: mm(x[ch], eye + pw[ch]) for ch in chains}
            k *= 2
        return x

    lhs4, bts, kls, v2s, bks, e_gls = {}, {}, {}, {}, {}, {}
    for bi in range(n_b):
        lw = lw_ref[bi]
        g_incl = jnp.dot(tri, lw, precision=_HI, preferred_element_type=F32)
        e_g = jnp.exp(g_incl)
        e_ng = jnp.exp(-g_incl)
        e_gl = e_g[L - 1:L, :]
        kt = kk_ref[bi] * jnp.exp(g_incl - lw)
        rt = r_ref[bi] * e_g
        bt = b_ref[bi] * e_ng
        kl = k_ref[bi] * e_ng
        bh = bt * e_gl
        kh = kl * e_gl
        vv = v_ref[bi]
        for p in range(N_PAIRS):
            sl = slice(p * LANES, (p + 1) * LANES)
            lhs4[bi, p] = jnp.concatenate([split(kt[:, sl]), split(rt[:, sl])], axis=0)
            bts[bi, p] = split(bt[:, sl])
            kls[bi, p] = split(kl[:, sl])
            v2s[bi, p] = split(vv[:, sl])
            bks[bi, p] = jnp.concatenate([split(bh[:, sl]), split(kh[:, sl])], axis=0)
            e_gls[bi, p] = e_gl[:, sl]

    ab = {ch: _dot_nt(lhs4[ch], bts[ch]) for ch in chains}
    ak = {ch: _dot_nt(lhs4[ch], kls[ch]) for ch in chains}
    a_kb = {ch: jnp.where(strict, ab[ch][0:2 * L], 0.0) for ch in chains}
    a_rb = {ch: jnp.where(incl, ab[ch][2 * L:4 * L], 0.0).astype(BF16) for ch in chains}
    a_kk = {ch: jnp.where(strict, ak[ch][0:2 * L], 0.0).astype(BF16) for ch in chains}
    a_rk = {ch: jnp.where(incl, ak[ch][2 * L:4 * L], 0.0).astype(BF16) for ch in chains}
    if L > blk:
        a_d = {ch: jnp.where(diag_blk, a_kb[ch], 0.0) for ch in chains}
        x_d = neumann(a_d, blk)
        nn = {ch: mm(x_d[ch], a_kb[ch] - a_d[ch]) for ch in chains}
        x_n = neumann(nn, L // blk)
        x_f = {ch: mm(x_n[ch], x_d[ch]).astype(BF16) for ch in chains}
    else:
        x_f = {ch: v.astype(BF16) for ch, v in neumann(a_kb, L).items()}

    s_old = {ch: sbd[ch[0], ch[1]] for ch in chains}
    pp = {ch: _dot_nt(lhs4[ch], s_old[ch].astype(BF16)) for ch in chains}
    rhs = {ch: -(pp[ch][0:2 * L] + _dot(a_kk[ch], v2s[ch])) for ch in chains}
    u2 = {ch: _dot(x_f[ch], rhs[ch].astype(BF16)).astype(BF16) for ch in chains}
    for ch in chains:
        bi, p = ch
        ys = pp[ch][2 * L:4 * L] + _dot(a_rb[ch], u2[ch]) + _dot(a_rk[ch], v2s[ch])
        y_ref[bi, :, p * LANES:(p + 1) * LANES] = ys[0:L] + ys[L:2 * L]
        upd = _dot_tn(jnp.concatenate([u2[ch], v2s[ch]], axis=0), bks[ch])
        sbd[bi, p] = s_old[ch] * e_gls[ch] + upd

    @pl.when(c == nc - 1)
    def _():
        for bi, p in chains:
            s_p = sbd[bi, p]
            so_ref[bi, 2 * p] = s_p[0:half, 0:half]
            so_ref[bi, 2 * p + 1] = pltpu.roll(s_p[half:2 * half, :], half, axis=1)[:, 0:half]


def _wkv(seqs, state0, chunk, rows_per_step):
    bsz, t, _ = seqs[0].shape
    zero_init = state0 is None
    nb = rows_per_step
    tok = pl.BlockSpec((nb, chunk, D_MODEL), lambda b, c: (b, c, 0))
    st = pl.BlockSpec((nb, N_HEADS, HEAD_DIM, HEAD_DIM), lambda b, c: (b, 0, 0, 0))
    args = list(seqs) + ([] if zero_init else [state0])
    specs = [tok] * 6 + ([] if zero_init else [st])
    return pl.pallas_call(
        functools.partial(_wkv_body, zero_init, chunk),
        out_shape=[jax.ShapeDtypeStruct((bsz, t, D_MODEL), F32),
                   jax.ShapeDtypeStruct((bsz, N_HEADS, HEAD_DIM, HEAD_DIM), F32)],
        grid=(bsz // nb, t // chunk),
        in_specs=specs,
        out_specs=[tok, st],
        scratch_shapes=[pltpu.VMEM((nb, N_PAIRS, LANES, LANES), F32)],
        compiler_params=pltpu.CompilerParams(dimension_semantics=("arbitrary", "arbitrary"),
                                             vmem_limit_bytes=VMEM_LIMIT),
        name="wkv",
    )(*args)


def _a_post_body(y_ref, r_ref, k_ref, v_ref, z_ref, x_ref, rk_ref, gw_ref, gb_ref, wo_ref, o_ref):
    bo = _block_ones()
    y = y_ref[...]
    inv_c = 1.0 / HEAD_DIM
    yc = y - _head_sum(y, bo) * inv_c
    var = _head_sum(yc * yc, bo) * inv_c
    yn = yc * lax.rsqrt(var + GN_EPS) * gw_ref[...] + gb_ref[...]
    yn = yn + _head_sum(r_ref[...] * k_ref[...] * rk_ref[...], bo) * v_ref[...]
    z = z_ref[...]
    gated = yn * (z * _sigmoid(z))
    o_ref[...] = x_ref[...] + _dot(gated.astype(BF16), wo_ref[...])


def _a_post(y, r, k, v, z, x, w):
    n = x.shape[0]
    tm = 512
    row = pl.BlockSpec((tm, D_MODEL), lambda i: (i, 0))
    vec = pl.BlockSpec((1, D_MODEL), lambda i: (0, 0))
    return pl.pallas_call(
        _a_post_body,
        out_shape=jax.ShapeDtypeStruct((n, D_MODEL), F32),
        grid=(n // tm,),
        in_specs=[row] * 6 + [vec] * 3 + [pl.BlockSpec((D_MODEL, D_MODEL), lambda i: (0, 0))],
        out_specs=row,
        compiler_params=pltpu.CompilerParams(dimension_semantics=("arbitrary",), vmem_limit_bytes=VMEM_LIMIT),
        name="a_post",
    )(y, r, k, v, z, x, w['r_k'], w['gn_w'], w['gn_b'], w['w_out'])


def _rope_tile(t, cos_t, sin_t):
    lane = lax.broadcasted_iota(jnp.int32, (1, LANES), 1)
    first_half = (lane % HEAD_DIM) < (HEAD_DIM // 2)
    outs = []
    for g in range(t.shape[1] // LANES):
        xg = t[:, g * LANES:(g + 1) * LANES]
        partner = jnp.where(first_half, pltpu.roll(xg, LANES - HEAD_DIM // 2, axis=1),
                            pltpu.roll(xg, HEAD_DIM // 2, axis=1))
        outs.append(xg * cos_t + partner * sin_t)
    return jnp.concatenate(outs, axis=1)


def _proj_body(n_rope, x_ref, ln_ref, w_ref, cos_ref, sin_ref, o_ref, xn_scr):
    j = pl.program_id(1)

    @pl.when(j == 0)
    def _():
        xn_scr[...] = _rms(x_ref[...], ln_ref[...]).astype(BF16)

    t = _dot(xn_scr[...], w_ref[...])

    @pl.when(j < n_rope)
    def _():
        o_ref[...] = _rope_tile(t, cos_ref[...], sin_ref[...])

    @pl.when(j >= n_rope)
    def _():
        o_ref[...] = t


def _proj(x, ln, w, cos_t, sin_t, n_rope):
    n = x.shape[0]
    n_out = w.shape[1]
    tm = min(512, n)
    tn = D_MODEL
    n_tab = cos_t.shape[0] // tm
    tab = pl.BlockSpec((tm, LANES), lambda i, j: (i % n_tab, 0))
    return pl.pallas_call(
        functools.partial(_proj_body, n_rope),
        out_shape=jax.ShapeDtypeStruct((n, n_out), F32),
        grid=(n // tm, n_out // tn),
        in_specs=[pl.BlockSpec((tm, D_MODEL), lambda i, j: (i, 0)),
                  pl.BlockSpec((1, D_MODEL), lambda i, j: (0, 0)),
                  pl.BlockSpec((D_MODEL, tn), lambda i, j: (0, j)),
                  tab, tab],
        out_specs=pl.BlockSpec((tm, tn), lambda i, j: (i, j)),
        scratch_shapes=[pltpu.VMEM((tm, D_MODEL), BF16)],
        compiler_params=pltpu.CompilerParams(dimension_semantics=("arbitrary", "arbitrary"),
                                             vmem_limit_bytes=VMEM_LIMIT),
        name="proj",
    )(x, ln, w, cos_t, sin_t)


def _attn_prompt_body(seq_len, q0, q1, q2, k0, k1, k2, v0, v1, v2, o_ref, m_s, l_s, a_s):
    nq = KEYS_PER_QUERY
    lane = lax.broadcasted_iota(jnp.int32, (1, LANES), 1)
    h0 = lane < HEAD_DIM
    qi = lax.broadcasted_iota(jnp.int32, (nq, nq), 0)
    ki = lax.broadcasted_iota(jnp.int32, (nq, nq), 1)
    cur_mask = ki <= qi
    prev_mask = ki > qi

    heads = (h0, jnp.logical_not(h0))

    def run_blocks(g, q_ref, k_ref, v_ref, d, blocks):
        rows, qs, kbs, vbs, masks = [], [], [], [], []
        for c, nb, prev in blocks:
            rw = pl.ds(c + d * nq * nb, nq, stride=d)
            kb = k_ref[rw, :].astype(BF16)
            vb = v_ref[rw, :].astype(BF16)
            mask = cur_mask
            if prev is not None:
                pb = nb - 1 if prev is True else jnp.maximum(nb - 1, 0)
                prow = pl.ds(c + d * nq * pb, nq, stride=d)
                kb = jnp.concatenate([k_ref[prow, :].astype(BF16), kb], axis=0)
                vb = jnp.concatenate([v_ref[prow, :].astype(BF16), vb], axis=0)
                pm = prev_mask if prev is True else jnp.logical_and(prev_mask, prev)
                mask = jnp.concatenate([pm, cur_mask], axis=1)
            rows.append(rw)
            qs.append(q_ref[rw, :] * ATT_SCALE)
            kbs.append(kb)
            vbs.append(vb)
            masks.append(mask)
        n = len(blocks)
        pairs = [(i, h) for i in range(n) for h in range(2)]
        s = {(i, h): _dot_nt(jnp.where(heads[h], qs[i], 0.0).astype(BF16), kbs[i]) for i, h in pairs}
        s = {ih: jnp.where(masks[ih[0]], s[ih], NEG_INF) for ih in pairs}
        mx = {ih: jnp.max(s[ih], axis=-1, keepdims=True) for ih in pairs}
        pr = {ih: jnp.exp(s[ih] - mx[ih]) for ih in pairs}
        ls = {ih: jnp.sum(pr[ih], axis=-1, keepdims=True) for ih in pairs}
        acc = {(i, h): _dot(pr[i, h].astype(BF16), vbs[i]) for i, h in pairs}
        for i in range(n):
            m_b = jnp.where(h0, mx[i, 0], mx[i, 1])
            l_b = jnp.where(h0, ls[i, 0], ls[i, 1])
            a_b = jnp.where(h0, acc[i, 0], acc[i, 1])
            if g > 0:
                m_o, l_o, a_o = m_s[rows[i], :], l_s[rows[i], :], a_s[rows[i], :]
                m_n = jnp.maximum(m_o, m_b)
                w_o = jnp.exp(m_o - m_n)
                w_b = jnp.exp(m_b - m_n)
                m_b = m_n
                l_b = w_o * l_o + w_b * l_b
                a_b = w_o * a_o + w_b * a_b
            if g == N_GROUPS - 1:
                o_ref[rows[i], :] = a_b / l_b
            else:
                m_s[rows[i], :] = m_b
                l_s[rows[i], :] = l_b
                a_s[rows[i], :] = a_b

    u = ATTN_UNROLL
    for g, (q_ref, k_ref, v_ref) in enumerate(((q0, k0, v0), (q1, k1, v1), (q2, k2, v2))):
        d = DILATIONS[g]
        n_blk = seq_len // (d * nq)
        if n_blk == 1:
            def classes(i, carry, g=g, q_ref=q_ref, k_ref=k_ref, v_ref=v_ref, d=d):
                run_blocks(g, q_ref, k_ref, v_ref, d, [(i * u + j, 0, None) for j in range(u)])
                return carry
            lax.fori_loop(0, d // u, classes, 0)
        elif n_blk == u:
            def one_class(c, carry, g=g, q_ref=q_ref, k_ref=k_ref, v_ref=v_ref, d=d):
                run_blocks(g, q_ref, k_ref, v_ref, d, [(c, nb, True if nb else None) for nb in range(u)])
                return carry
            lax.fori_loop(0, d, one_class, 0)
        else:
            def blocks(i, carry, g=g, q_ref=q_ref, k_ref=k_ref, v_ref=v_ref, d=d):
                run_blocks(g, q_ref, k_ref, v_ref, d,
                           [(0, i * u + j, (i > 0) if j == 0 else True) for j in range(u)])
                return carry
            assert d == 1 and n_blk % u == 0
            lax.fori_loop(0, n_blk // u, blocks, 0)


def _attn_prompt(proj, kv, bsz, seq_len):
    proj3 = proj.reshape(bsz, seq_len, proj.shape[1])
    kv3 = kv.reshape(bsz, seq_len, kv.shape[1])

    def col(base):
        return pl.BlockSpec((None, seq_len, LANES), lambda b, p: (b, 0, base + p))

    q_specs = [col(g * N_PAIRS) for g in range(N_GROUPS)]
    k_specs = [col(g * N_PAIRS) for g in range(N_GROUPS)]
    v_specs = [col((N_GROUPS + g) * N_PAIRS) for g in range(N_GROUPS)]
    out = pl.pallas_call(
        functools.partial(_attn_prompt_body, seq_len),
        out_shape=jax.ShapeDtypeStruct((bsz, seq_len, D_MODEL), F32),
        grid=(bsz, N_PAIRS),
        in_specs=q_specs + k_specs + v_specs,
        out_specs=pl.BlockSpec((None, seq_len, LANES), lambda b, p: (b, 0, p)),
        scratch_shapes=[pltpu.VMEM((seq_len, LANES), F32)] * 3,
        compiler_params=pltpu.CompilerParams(dimension_semantics=("arbitrary", "arbitrary"),
                                             vmem_limit_bytes=VMEM_LIMIT),
        name="attn_prompt",
    )(proj3, proj3, proj3, kv3, kv3, kv3, kv3, kv3, kv3)
    return out.reshape(bsz * seq_len, D_MODEL)


def _attn_sample_body(n_new, q_ref, kvn_ref, c0_ref, c1_ref, c2_ref, o_ref):
    rowid = lax.broadcasted_iota(jnp.int32, (KEYS_PER_QUERY, 1, 1), 0)
    newid = lax.broadcasted_iota(jnp.int32, (n_new, 1, 1), 0)
    caches = (c0_ref, c1_ref, c2_ref)
    for j in range(n_new):
        m_run = l_run = a_run = None
        for g in range(N_GROUPS):
            d = DILATIONS[g]
            heads = slice(g * N_HEADS, (g + 1) * N_HEADS)
            v_heads = slice((N_GROUPS + g) * N_HEADS, (N_GROUPS + g + 1) * N_HEADS)
            qj = q_ref[j, heads, :] * ATT_SCALE
            k_new = kvn_ref[:, heads, :]
            v_new = kvn_ref[:, v_heads, :]
            res = j % d
            first_row = j // d + 1
            k_c = caches[g][:, res, 0]
            v_c = caches[g][:, res, 1]
            new_ok = (newid <= j) & ((j - newid) % d == 0)
            s_c = jnp.where(rowid >= first_row, jnp.sum(k_c * qj[None], axis=-1, keepdims=True), NEG_INF)
            s_n = jnp.where(new_ok, jnp.sum(k_new * qj[None], axis=-1, keepdims=True), NEG_INF)
            m_g = jnp.maximum(jnp.max(s_c, axis=0, keepdims=True), jnp.max(s_n, axis=0, keepdims=True))
            e_c = jnp.exp(s_c - m_g)
            e_n = jnp.exp(s_n - m_g)
            l_g = jnp.sum(e_c, axis=0, keepdims=True) + jnp.sum(e_n, axis=0, keepdims=True)
            a_g = jnp.sum(e_c * v_c, axis=0, keepdims=True) + jnp.sum(e_n * v_new, axis=0, keepdims=True)
            if g == 0:
                m_run, l_run, a_run = m_g, l_g, a_g
            else:
                m_n = jnp.maximum(m_run, m_g)
                w_o = jnp.exp(m_run - m_n)
                w_g = jnp.exp(m_g - m_n)
                m_run = m_n
                l_run = w_o * l_run + w_g * l_g
                a_run = w_o * a_run + w_g * a_g
        o_ref[j] = (a_run / l_run)[0]


def _attn_sample(proj, kv, caches, bsz, n_new):
    q4 = proj.reshape(bsz, n_new, proj.shape[1] // HEAD_DIM, HEAD_DIM)
    kv4 = kv.reshape(bsz, n_new, kv.shape[1] // HEAD_DIM, HEAD_DIM)
    views, specs = [], []
    for g in range(N_GROUPS):
        d = DILATIONS[g]
        assert caches[g].shape[1] == WINDOWS[g] and (d == 1 or n_new <= d)
        views.append(caches[g].reshape(bsz, KEYS_PER_QUERY, d, 2, N_HEADS, HEAD_DIM))
        n_res = min(d, n_new)
        specs.append(pl.BlockSpec((None, KEYS_PER_QUERY, n_res, 2, N_HEADS, HEAD_DIM),
                                  lambda b: (b, 0, 0, 0, 0, 0)))
    out = pl.pallas_call(
        functools.partial(_attn_sample_body, n_new),
        out_shape=jax.ShapeDtypeStruct((bsz, n_new, N_HEADS, HEAD_DIM), F32),
        grid=(bsz,),
        in_specs=[pl.BlockSpec((None, n_new) + q4.shape[2:], lambda b: (b, 0, 0, 0)),
                  pl.BlockSpec((None, n_new) + kv4.shape[2:], lambda b: (b, 0, 0, 0))] + specs,
        out_specs=pl.BlockSpec((None, n_new, N_HEADS, HEAD_DIM), lambda b: (b, 0, 0, 0)),
        compiler_params=pltpu.CompilerParams(dimension_semantics=("arbitrary",), vmem_limit_bytes=VMEM_LIMIT),
        name="attn_sample",
    )(q4, kv4, *views)
    return out.reshape(bsz * n_new, D_MODEL)


def _b_post_body(final, *refs):
    if final:
        o_ref, z_ref, x_ref, wo_ref, fl_ref, out_ref = refs
    else:
        o_ref, z_ref, x_ref, wo_ref, out_ref = refs
    z = z_ref[...]
    gated = o_ref[...] * (z * _sigmoid(z))
    h = x_ref[...] + _dot(gated.astype(BF16), wo_ref[...])
    out_ref[...] = _rms(h, fl_ref[...]) if final else h


def _b_post(o, proj, x, w_out, final_ln):
    n = x.shape[0]
    tm = min(512, n)
    final = final_ln is not None
    row = pl.BlockSpec((tm, D_MODEL), lambda i: (i, 0))
    specs = [row, pl.BlockSpec((tm, D_MODEL), lambda i: (i, N_GROUPS)), row,
             pl.BlockSpec((D_MODEL, D_MODEL), lambda i: (0, 0))]
    args = [o, proj, x, w_out]
    if final:
        specs.append(pl.BlockSpec((1, D_MODEL), lambda i: (0, 0)))
        args.append(final_ln)
    return pl.pallas_call(
        functools.partial(_b_post_body, final),
        out_shape=jax.ShapeDtypeStruct((n, D_MODEL), F32),
        grid=(n // tm,),
        in_specs=specs,
        out_specs=row,
        compiler_params=pltpu.CompilerParams(dimension_semantics=("arbitrary",), vmem_limit_bytes=VMEM_LIMIT),
        name="b_post",
    )(*args)


def _rope_tables(pos):
    half = HEAD_DIM // 2
    inv = ROPE_THETA ** (-jnp.arange(half, dtype=F32) / half)
    ang = pos.astype(F32)[:, None] * inv[None, :]
    cos, sin = jnp.cos(ang), jnp.sin(ang)
    return jnp.concatenate([cos, cos, cos, cos], axis=1), jnp.concatenate([-sin, sin, -sin, sin], axis=1)


def _rwkv_stack(x, shift_init, wkv_init, n_seq, seq_len, layers):
    shifts, states = [], []
    v_first = None
    if seq_len % WKV_CHUNK == 0:
        chunk, rows_per_step, pad = WKV_CHUNK, 2, 0
    else:
        chunk, rows_per_step = WKV_SMALL_CHUNK, 2
        pad = (-seq_len) % chunk
    for l, w in enumerate(layers):
        (r, lw, k, v, kk, b, z), sh = _a_pre(x, shift_init[l], v_first, w, n_seq, seq_len)
        if l == 0:
            v_first = v
        seqs = [t.reshape(n_seq, seq_len, D_MODEL) for t in (r, lw, k, v, kk, b)]
        if pad:
            seqs = [jnp.pad(t, ((0, 0), (0, pad), (0, 0))) for t in seqs]
        y, st = _wkv(seqs, None if wkv_init is None else wkv_init[l], chunk, rows_per_step)
        y = y[:, :seq_len].reshape(n_seq * seq_len, D_MODEL)
        x = _a_post(y, r, k, v, z, x, w)
        shifts.append(sh)
        states.append(st)
    return x, jnp.stack(shifts), jnp.stack(states)


def kernel(x_prompt, x_sample, state_wkv, state_shift, cache_kv_g0, cache_kv_g1, cache_kv_g2, a_ln, a_mu, a_w_in, a_w0, a_w_lora_a, a_w_lora_b, a_a0, a_a_lora_a, a_a_lora_b, a_v0, a_v_lora_a, a_v_lora_b, a_k_k, a_k_a, a_r_k, a_gn_w, a_gn_b, a_w_out, kv_ln, w_kv, b_ln, b_w_in, b_w_out, final_ln):
    bp, s, _ = x_prompt.shape
    bs, t, _ = x_sample.shape
    n_a = a_ln.shape[0]
    n_b = b_ln.shape[0]

    def row(vec):
        return vec.reshape(1, D_MODEL)

    layers = []
    for l in range(n_a):
        w = dict(ln=row(a_ln[l]), mu=a_mu[l], w_in=a_w_in[l].astype(BF16), w0=row(a_w0[l]),
                 w_la=a_w_lora_a[l].astype(BF16), w_lb=a_w_lora_b[l].astype(BF16), a0=row(a_a0[l]),
                 a_la=a_a_lora_a[l].astype(BF16), a_lb=a_a_lora_b[l].astype(BF16),
                 k_k=row(a_k_k[l]), k_a=row(a_k_a[l]), r_k=row(a_r_k[l]), gn_w=row(a_gn_w[l]),
                 gn_b=row(a_gn_b[l]), w_out=a_w_out[l].astype(BF16))
        if l > 0:
            w.update(v0=row(a_v0[l - 1]), v_la=a_v_lora_a[l - 1].astype(BF16), v_lb=a_v_lora_b[l - 1].astype(BF16))
        layers.append(w)

    xp = x_prompt.reshape(bp * s, D_MODEL)
    xs = x_sample.reshape(bs * t, D_MODEL)
    zero_shift = jnp.zeros((n_a, bp, D_MODEL), F32)
    hp, shift_p, wkv_p = _rwkv_stack(xp, zero_shift, None, bp, s, layers)
    hs, shift_s, wkv_s = _rwkv_stack(xs, state_shift, state_wkv, bs, t, layers)

    cos_p, sin_p = _rope_tables(jnp.arange(s, dtype=jnp.int32))
    cos_s, sin_s = _rope_tables(jnp.tile(PAST_LEN + jnp.arange(t, dtype=jnp.int32), bs))
    w_kv_b = w_kv.astype(BF16)
    kv_ln_r = row(kv_ln)
    kvp = _proj(hp, kv_ln_r, w_kv_b, cos_p, sin_p, N_GROUPS)
    kvs = _proj(hs, kv_ln_r, w_kv_b, cos_s, sin_s, N_GROUPS)

    caches = (cache_kv_g0, cache_kv_g1, cache_kv_g2)
    for l in range(n_b):
        w_in = b_w_in[l].astype(BF16)
        w_out = b_w_out[l].astype(BF16)
        fl = row(final_ln) if l == n_b - 1 else None
        pj = _proj(hp, row(b_ln[l]), w_in, cos_p, sin_p, N_GROUPS)
        hp = _b_post(_attn_prompt(pj, kvp, bp, s), pj, hp, w_out, fl)
        pj = _proj(hs, row(b_ln[l]), w_in, cos_s, sin_s, N_GROUPS)
        hs = _b_post(_attn_sample(pj, kvs, caches, bs, t), pj, hs, w_out, fl)

    y_prompt = hp.reshape(bp, s, D_MODEL)
    y_sample = hs.reshape(bs, t, D_MODEL)

    kvp3 = kvp.reshape(bp, s, 2, N_GROUPS, N_HEADS, HEAD_DIM)
    kvs3 = kvs.reshape(bs, t, 2, N_GROUPS, N_HEADS, HEAD_DIM)
    kv_out = []
    for g in range(N_GROUPS):
        wc = min(WINDOWS[g], s)
        kv_out.append(kvp3[:, s - wc:, :, g])
        kv_out.append(kvs3[:, :, :, g])
    return (y_prompt, y_sample, wkv_p, wkv_s, shift_p, shift_s, *kv_out)
```

```python
import functools

import jax
import jax.numpy as jnp
from jax import lax
from jax.experimental import pallas as pl
from jax.experimental.pallas import tpu as pltpu

F32 = jnp.float32
BF16 = jnp.bfloat16

D_MODEL = 1024
HEAD_DIM = 64
N_HEADS = D_MODEL // HEAD_DIM
LANES = 128
N_PAIRS = D_MODEL // LANES
N_GROUPS = 3
WINDOWS = (128, 512, 2048)
DILATIONS = (1, 4, 16)
KEYS_PER_QUERY = 128
ATT_SCALE = HEAD_DIM ** -0.5
ROPE_THETA = 10000.0
NEG_INF = -1e30
GN_EPS = 64e-5
RMS_EPS = 1e-6
PAST_LEN = 2048
WKV_CHUNK = 64
WKV_SMALL_CHUNK = 8
ATTN_UNROLL = 4
ATTN_S_SPLIT = 2
VMEM_LIMIT = 56 * 1024 * 1024

_HI = lax.Precision.HIGHEST


def _dot(a, b):
    return jnp.dot(a, b, preferred_element_type=F32)


def _dot_nt(a, b):
    return lax.dot_general(a, b, (((1,), (1,)), ((), ())), preferred_element_type=F32)


def _dot_tn(a, b):
    return lax.dot_general(a, b, (((0,), (0,)), ((), ())), preferred_element_type=F32)


def _rms(x, g):
    return x * lax.rsqrt(jnp.mean(x * x, axis=-1, keepdims=True) + RMS_EPS) * g


def _sigmoid(x):
    return 1.0 / (1.0 + jnp.exp(-x))


def _block_ones():
    r = lax.broadcasted_iota(jnp.int32, (LANES, LANES), 0) // HEAD_DIM
    c = lax.broadcasted_iota(jnp.int32, (LANES, LANES), 1) // HEAD_DIM
    return (r == c).astype(BF16)


def _head_sum(x, bo):
    outs = []
    for g in range(x.shape[1] // LANES):
        xg = x[:, g * LANES:(g + 1) * LANES]
        hi = xg.astype(BF16)
        lo = (xg - hi.astype(F32)).astype(BF16)
        outs.append(_dot(hi, bo) + _dot(lo, bo))
    return outs[0] if len(outs) == 1 else jnp.concatenate(outs, axis=1)


def _a_pre_body(has_vres, per_row_shift, seq_len, *refs):
    it = iter(refs)
    x_ref, sp_ref = next(it), next(it)
    vf_ref = next(it) if has_vres else None
    ln_ref, mu_ref, win_ref = next(it), next(it), next(it)
    w0_ref, wla_ref, wlb_ref = next(it), next(it), next(it)
    a0_ref, ala_ref, alb_ref = next(it), next(it), next(it)
    if has_vres:
        v0_ref, vla_ref, vlb_ref = next(it), next(it), next(it)
    kk_ref, ka_ref = next(it), next(it)
    r_o, lw_o, k_o, v_o, kk_o, b_o, z_o, sh_o = [next(it) for _ in range(8)]
    scr = next(it)

    tm = x_ref.shape[0]
    xn = _rms(x_ref[...], ln_ref[...])
    rolled = pltpu.roll(xn, 1, axis=0)
    row = lax.broadcasted_iota(jnp.int32, (tm, 1), 0)
    if per_row_shift:
        prev = jnp.where(row % seq_len == 0, sp_ref[...], rolled)
        for g in range(N_PAIRS):
            lanes = slice(g * LANES, (g + 1) * LANES)
            scr[g] = xn[:, lanes]
            sh_o[:, lanes] = scr[g, pl.ds(seq_len - 1, tm // seq_len, stride=seq_len), :]
    else:
        first = jnp.where(pl.program_id(1) == 0, sp_ref[...], scr[0:1, :])
        prev = jnp.where(row == 0, first, rolled)
        scr[0:1, :] = xn[tm - 1:tm, :]
        sh_o[...] = xn[tm - 1:tm, :]

    dx = prev - xn
    mu = mu_ref[...]

    def mixed(p):
        return xn + mu[p:p + 1, :] * dx

    xv = mixed(2)
    r = _dot(mixed(0).astype(BF16), win_ref[0])
    k = _dot(mixed(1).astype(BF16), win_ref[1])
    v = _dot(xv.astype(BF16), win_ref[2])
    z_o[...] = _dot(mixed(3).astype(BF16), win_ref[3])

    def lora(xm, la_ref, lb_ref, act):
        hdn = _dot(xm.astype(BF16), la_ref[...])
        if act:
            hdn = jnp.tanh(hdn)
        return _dot(hdn.astype(BF16), lb_ref[...])

    wl = w0_ref[...] + lora(mixed(4), wla_ref, wlb_ref, True)
    sp = jnp.maximum(-wl, 0.0) + jnp.log1p(jnp.exp(-jnp.abs(wl)))
    lw_o[...] = -jnp.exp(-sp - 0.5)
    a = _sigmoid(a0_ref[...] + lora(mixed(5), ala_ref, alb_ref, False))
    if has_vres:
        v = v + (vf_ref[...] - v) * _sigmoid(v0_ref[...] + lora(xv, vla_ref, vlb_ref, False))

    bo = _block_ones()
    kkp = k * kk_ref[...]
    kk = kkp * lax.rsqrt(jnp.maximum(_head_sum(kkp * kkp, bo), 1e-24))
    r_o[...] = r
    k_o[...] = k * (1.0 + (a - 1.0) * ka_ref[...])
    v_o[...] = v
    kk_o[...] = kk
    b_o[...] = kk * a


def _a_pre(x, shift_prev, v_first, w, n_seq, seq_len):
    n = x.shape[0]
    has_vres = v_first is not None
    per_row_shift = seq_len < 256
    if per_row_shift:
        tm = n
        grid = (1, 1)
        sp = jnp.repeat(shift_prev, seq_len, axis=0)
        sp_spec = pl.BlockSpec((tm, D_MODEL), lambda b, i: (0, 0))
        sh_shape = jax.ShapeDtypeStruct((n_seq, D_MODEL), F32)
        sh_spec = pl.BlockSpec((n_seq, D_MODEL), lambda b, i: (0, 0))
        scr = pltpu.VMEM((N_PAIRS, tm, LANES), F32)
        tiles = 1
    else:
        tm = 256
        tiles = seq_len // tm
        grid = (n_seq, tiles)
        sp = shift_prev.reshape(n_seq, 1, D_MODEL)
        sp_spec = pl.BlockSpec((None, 1, D_MODEL), lambda b, i: (b, 0, 0))
        sh_shape = jax.ShapeDtypeStruct((n_seq, 1, D_MODEL), F32)
        sh_spec = pl.BlockSpec((None, 1, D_MODEL), lambda b, i: (b, 0, 0))
        scr = pltpu.VMEM((8, D_MODEL), F32)

    row_spec = pl.BlockSpec((tm, D_MODEL), lambda b, i: (b * tiles + i, 0))

    def const(shape):
        return pl.BlockSpec(shape, lambda b, i: (0,) * len(shape))

    vec = const((1, D_MODEL))
    args = [x, sp]
    specs = [row_spec, sp_spec]
    if has_vres:
        args.append(v_first)
        specs.append(row_spec)
    args += [w['ln'], w['mu'], w['w_in'], w['w0'], w['w_la'], w['w_lb'], w['a0'], w['a_la'], w['a_lb']]
    specs += [vec, const((6, D_MODEL)), const((4, D_MODEL, D_MODEL)), vec, const(w['w_la'].shape),
              const(w['w_lb'].shape), vec, const(w['a_la'].shape), const(w['a_lb'].shape)]
    if has_vres:
        args += [w['v0'], w['v_la'], w['v_lb']]
        specs += [vec, const(w['v_la'].shape), const(w['v_lb'].shape)]
    args += [w['k_k'], w['k_a']]
    specs += [vec, vec]

    full = jax.ShapeDtypeStruct((n, D_MODEL), F32)
    outs = pl.pallas_call(
        functools.partial(_a_pre_body, has_vres, per_row_shift, seq_len),
        out_shape=[full] * 7 + [sh_shape],
        grid=grid,
        in_specs=specs,
        out_specs=[row_spec] * 7 + [sh_spec],
        scratch_shapes=[scr],
        compiler_params=pltpu.CompilerParams(dimension_semantics=("arbitrary", "arbitrary"),
                                             vmem_limit_bytes=VMEM_LIMIT),
        name="a_pre",
    )(*args)
    return outs[:7], outs[7].reshape(n_seq, D_MODEL)


def _wkv_body(zero_init, L, *refs):
    it = iter(refs)
    r_ref, lw_ref, k_ref, v_ref, kk_ref, b_ref = [next(it) for _ in range(6)]
    st_ref = None if zero_init else next(it)
    y_ref, so_ref = next(it), next(it)
    sbd = next(it)

    n_b = r_ref.shape[0]
    c = pl.program_id(1)
    nc = pl.num_programs(1)
    half = HEAD_DIM
    chains = [(bi, p) for bi in range(n_b) for p in range(N_PAIRS)]

    @pl.when(c == 0)
    def _():
        if zero_init:
            sbd[...] = jnp.zeros(sbd.shape, F32)
        else:
            zeros = jnp.zeros((half, half), F32)
            for bi, p in chains:
                top = jnp.concatenate([st_ref[bi, 2 * p], zeros], axis=1)
                bot = jnp.concatenate([zeros, st_ref[bi, 2 * p + 1]], axis=1)
                sbd[bi, p] = jnp.concatenate([top, bot], axis=0)

    rr = lax.broadcasted_iota(jnp.int32, (L, L), 0)
    cc = lax.broadcasted_iota(jnp.int32, (L, L), 1)
    tri = (rr >= cc).astype(F32)
    lane = lax.broadcasted_iota(jnp.int32, (1, LANES), 1)
    h0 = lane < half
    r2 = lax.broadcasted_iota(jnp.int32, (2 * L, 2 * L), 0)
    c2 = lax.broadcasted_iota(jnp.int32, (2 * L, 2 * L), 1)
    strict = r2 > c2
    incl = r2 >= c2
    eye = (r2 == c2).astype(F32)
    blk = min(16, L)
    diag_blk = (r2 // blk) == (c2 // blk)

    def split(x):
        return jnp.concatenate([jnp.where(h0, x, 0.0), jnp.where(h0, 0.0, x)], axis=0).astype(BF16)

    def mm(a, b):
        return _dot(a.astype(BF16), b.astype(BF16))

    def neumann(a, n_terms):
        x = {ch: eye - a[ch] for ch in chains}
        pw, k = a, 2
        while k < n_terms:
            pw = {ch: mm(pw[ch], pw[ch]) for ch in chains}
            x = {ch: mm(x[ch], eye + pw[ch]) for ch in chains}
            k *= 2
        return x

    lhs4, bts, kls, v2s, bks, e_gls = {}, {}, {}, {}, {}, {}
    for bi in range(n_b):
        lw = lw_ref[bi]
        g_incl = jnp.dot(tri, lw, precision=_HI, preferred_element_type=F32)
        e_g = jnp.exp(g_incl)
        e_ng = jnp.exp(-g_incl)
        e_gl = e_g[L - 1:L, :]
        kt = kk_ref[bi] * jnp.exp(g_incl - lw)
        rt = r_ref[bi] * e_g
        bt = b_ref[bi] * e_ng
        kl = k_ref[bi] * e_ng
        bh = bt * e_gl
        kh = kl * e_gl
        vv = v_ref[bi]
        for p in range(N_PAIRS):
            sl = slice(p * LANES, (p + 1) * LANES)
            lhs4[bi, p] = jnp.concatenate([split(kt[:, sl]), split(rt[:, sl])], axis=0)
            bts[bi, p] = split(bt[:, sl])
            kls[bi, p] = split(kl[:, sl])
            v2s[bi, p] = split(vv[:, sl])
            bks[bi, p] = jnp.concatenate([split(bh[:, sl]), split(kh[:, sl])], axis=0)
            e_gls[bi, p] = e_gl[:, sl]

    ab = {ch: _dot_nt(lhs4[ch], bts[ch]) for ch in chains}
    ak = {ch: _dot_nt(lhs4[ch], kls[ch]) for ch in chains}
    a_kb = {ch: jnp.where(strict, ab[ch][0:2 * L], 0.0) for ch in chains}
    a_rb = {ch: jnp.where(incl, ab[ch][2 * L:4 * L], 0.0).astype(BF16) for ch in chains}
    a_kk = {ch: jnp.where(strict, ak[ch][0:2 * L], 0.0).astype(BF16) for ch in chains}
    a_rk = {ch: jnp.where(incl, ak[ch][2 * L:4 * L], 0.0).astype(BF16) for ch in chains}
    if L > blk:
        a_d = {ch: jnp.where(diag_blk, a_kb[ch], 0.0) for ch in chains}
        x_d = neumann(a_d, blk)
        nn = {ch: mm(x_d[ch], a_kb[ch] - a_d[ch]) for ch in chains}
        x_n = neumann(nn, L // blk)
        x_f = {ch: mm(x_n[ch], x_d[ch]).astype(BF16) for ch in chains}
    else:
        x_f = {ch: v.astype(BF16) for ch, v in neumann(a_kb, L).items()}

    s_old = {ch: sbd[ch[0], ch[1]] for ch in chains}
    pp = {ch: _dot_nt(lhs4[ch], s_old[ch].astype(BF16)) for ch in chains}
    rhs = {ch: -(pp[ch][0:2 * L] + _dot(a_kk[ch], v2s[ch])) for ch in chains}
    u2 = {ch: _dot(x_f[ch], rhs[ch].astype(BF16)).astype(BF16) for ch in chains}
    for ch in chains:
        bi, p = ch
        ys = pp[ch][2 * L:4 * L] + _dot(a_rb[ch], u2[ch]) + _dot(a_rk[ch], v2s[ch])
        y_ref[bi, :, p * LANES:(p + 1) * LANES] = ys[0:L] + ys[L:2 * L]
        upd = _dot_tn(jnp.concatenate([u2[ch], v2s[ch]], axis=0), bks[ch])
        sbd[bi, p] = s_old[ch] * e_gls[ch] + upd

    @pl.when(c == nc - 1)
    def _():
        for bi, p in chains:
            s_p = sbd[bi, p]
            so_ref[bi, 2 * p] = s_p[0:half, 0:half]
            so_ref[bi, 2 * p + 1] = pltpu.roll(s_p[half:2 * half, :], half, axis=1)[:, 0:half]


def _wkv(seqs, state0, chunk, rows_per_step):
    bsz, t, _ = seqs[0].shape
    zero_init = state0 is None
    nb = rows_per_step
    tok = pl.BlockSpec((nb, chunk, D_MODEL), lambda b, c: (b, c, 0))
    st = pl.BlockSpec((nb, N_HEADS, HEAD_DIM, HEAD_DIM), lambda b, c: (b, 0, 0, 0))
    args = list(seqs) + ([] if zero_init else [state0])
    specs = [tok] * 6 + ([] if zero_init else [st])
    return pl.pallas_call(
        functools.partial(_wkv_body, zero_init, chunk),
        out_shape=[jax.ShapeDtypeStruct((bsz, t, D_MODEL), F32),
                   jax.ShapeDtypeStruct((bsz, N_HEADS, HEAD_DIM, HEAD_DIM), F32)],
        grid=(bsz // nb, t // chunk),
        in_specs=specs,
        out_specs=[tok, st],
        scratch_shapes=[pltpu.VMEM((nb, N_PAIRS, LANES, LANES), F32)],
        compiler_params=pltpu.CompilerParams(dimension_semantics=("arbitrary", "arbitrary"),
                                             vmem_limit_bytes=VMEM_LIMIT),
        name="wkv",
    )(*args)


def _a_post_body(y_ref, r_ref, k_ref, v_ref, z_ref, x_ref, rk_ref, gw_ref, gb_ref, wo_ref, o_ref):
    bo = _block_ones()
    y = y_ref[...]
    inv_c = 1.0 / HEAD_DIM
    yc = y - _head_sum(y, bo) * inv_c
    var = _head_sum(yc * yc, bo) * inv_c
    yn = yc * lax.rsqrt(var + GN_EPS) * gw_ref[...] + gb_ref[...]
    yn = yn + _head_sum(r_ref[...] * k_ref[...] * rk_ref[...], bo) * v_ref[...]
    z = z_ref[...]
    gated = yn * (z * _sigmoid(z))
    o_ref[...] = x_ref[...] + _dot(gated.astype(BF16), wo_ref[...])


def _a_post(y, r, k, v, z, x, w):
    n = x.shape[0]
    tm = 512
    row = pl.BlockSpec((tm, D_MODEL), lambda i: (i, 0))
    vec = pl.BlockSpec((1, D_MODEL), lambda i: (0, 0))
    return pl.pallas_call(
        _a_post_body,
        out_shape=jax.ShapeDtypeStruct((n, D_MODEL), F32),
        grid=(n // tm,),
        in_specs=[row] * 6 + [vec] * 3 + [pl.BlockSpec((D_MODEL, D_MODEL), lambda i: (0, 0))],
        out_specs=row,
        compiler_params=pltpu.CompilerParams(dimension_semantics=("arbitrary",), vmem_limit_bytes=VMEM_LIMIT),
        name="a_post",
    )(y, r, k, v, z, x, w['r_k'], w['gn_w'], w['gn_b'], w['w_out'])


def _rope_tile(t, cos_t, sin_t):
    lane = lax.broadcasted_iota(jnp.int32, (1, LANES), 1)
    first_half = (lane % HEAD_DIM) < (HEAD_DIM // 2)
    outs = []
    for g in range(t.shape[1] // LANES):
        xg = t[:, g * LANES:(g + 1) * LANES]
        partner = jnp.where(first_half, pltpu.roll(xg, LANES - HEAD_DIM // 2, axis=1),
                            pltpu.roll(xg, HEAD_DIM // 2, axis=1))
        outs.append(xg * cos_t + partner * sin_t)
    return jnp.concatenate(outs, axis=1)


def _proj_body(n_rope, x_ref, ln_ref, w_ref, cos_ref, sin_ref, o_ref, xn_scr):
    j = pl.program_id(1)

    @pl.when(j == 0)
    def _():
        xn_scr[...] = _rms(x_ref[...], ln_ref[...]).astype(BF16)

    t = _dot(xn_scr[...], w_ref[...])

    @pl.when(j < n_rope)
    def _():
        o_ref[...] = _rope_tile(t, cos_ref[...], sin_ref[...])

    @pl.when(j >= n_rope)
    def _():
        o_ref[...] = t


def _proj(x, ln, w, cos_t, sin_t, n_rope):
    n = x.shape[0]
    n_out = w.shape[1]
    tm = min(512, n)
    tn = D_MODEL
    n_tab = cos_t.shape[0] // tm
    tab = pl.BlockSpec((tm, LANES), lambda i, j: (i % n_tab, 0))
    return pl.pallas_call(
        functools.partial(_proj_body, n_rope),
        out_shape=jax.ShapeDtypeStruct((n, n_out), F32),
        grid=(n // tm, n_out // tn),
        in_specs=[pl.BlockSpec((tm, D_MODEL), lambda i, j: (i, 0)),
                  pl.BlockSpec((1, D_MODEL), lambda i, j: (0, 0)),
                  pl.BlockSpec((D_MODEL, tn), lambda i, j: (0, j)),
                  tab, tab],
        out_specs=pl.BlockSpec((tm, tn), lambda i, j: (i, j)),
        scratch_shapes=[pltpu.VMEM((tm, D_MODEL), BF16)],
        compiler_params=pltpu.CompilerParams(dimension_semantics=("arbitrary", "arbitrary"),
                                             vmem_limit_bytes=VMEM_LIMIT),
        name="proj",
    )(x, ln, w, cos_t, sin_t)


def _attn_prompt_body(seq_len, q0, q1, q2, k0, k1, k2, v0, v1, v2, o_ref, m_s, l_s, a_s):
    nq = KEYS_PER_QUERY
    lane = lax.broadcasted_iota(jnp.int32, (1, LANES), 1)
    h0 = lane < HEAD_DIM
    qi = lax.broadcasted_iota(jnp.int32, (nq, nq), 0)
    ki = lax.broadcasted_iota(jnp.int32, (nq, nq), 1)
    cur_mask = ki <= qi
    prev_mask = ki > qi

    heads = (h0, jnp.logical_not(h0))

    def run_blocks(g, q_ref, k_ref, v_ref, d, blocks):
        rows, qs, kbs, vbs, masks = [], [], [], [], []
        for c, nb, prev in blocks:
            rw = pl.ds(c + d * nq * nb, nq, stride=d)
            kb = k_ref[rw, :].astype(BF16)
            vb = v_ref[rw, :].astype(BF16)
            mask = cur_mask
            if prev is not None:
                pb = nb - 1 if prev is True else jnp.maximum(nb - 1, 0)
                prow = pl.ds(c + d * nq * pb, nq, stride=d)
                kb = jnp.concatenate([k_ref[prow, :].astype(BF16), kb], axis=0)
                vb = jnp.concatenate([v_ref[prow, :].astype(BF16), vb], axis=0)
                pm = prev_mask if prev is True else jnp.logical_and(prev_mask, prev)
                mask = jnp.concatenate([pm, cur_mask], axis=1)
            rows.append(rw)
            qs.append(q_ref[rw, :] * ATT_SCALE)
            kbs.append(kb)
            vbs.append(vb)
            masks.append(mask)
        n = len(blocks)
        pairs = [(i, h) for i in range(n) for h in range(2)]
        s = {(i, h): _dot_nt(jnp.where(heads[h], qs[i], 0.0).astype(BF16), kbs[i]) for i, h in pairs}
        s = {ih: jnp.where(masks[ih[0]], s[ih], NEG_INF) for ih in pairs}
        mx = {ih: jnp.max(s[ih], axis=-1, keepdims=True) for ih in pairs}
        pr = {ih: jnp.exp(s[ih] - mx[ih]) for ih in pairs}
        ls = {ih: jnp.sum(pr[ih], axis=-1, keepdims=True) for ih in pairs}
        acc = {(i, h): _dot(pr[i, h].astype(BF16), vbs[i]) for i, h in pairs}
        for i in range(n):
            m_b = jnp.where(h0, mx[i, 0], mx[i, 1])
            l_b = jnp.where(h0, ls[i, 0], ls[i, 1])
            a_b = jnp.where(h0, acc[i, 0], acc[i, 1])
            if g > 0:
                m_o, l_o, a_o = m_s[rows[i], :], l_s[rows[i], :], a_s[rows[i], :]
                m_n = jnp.maximum(m_o, m_b)
                w_o = jnp.exp(m_o - m_n)
                w_b = jnp.exp(m_b - m_n)
                m_b = m_n
                l_b = w_o * l_o + w_b * l_b
                a_b = w_o * a_o + w_b * a_b
            if g == N_GROUPS - 1:
                o_ref[rows[i], :] = a_b / l_b
            else:
                m_s[rows[i], :] = m_b
                l_s[rows[i], :] = l_b
                a_s[rows[i], :] = a_b

    u = ATTN_UNROLL
    for g, (q_ref, k_ref, v_ref) in enumerate(((q0, k0, v0), (q1, k1, v1), (q2, k2, v2))):
        d = DILATIONS[g]
        n_blk = seq_len // (d * nq)
        if n_blk == 1:
            def classes(i, carry, g=g, q_ref=q_ref, k_ref=k_ref, v_ref=v_ref, d=d):
                run_blocks(g, q_ref, k_ref, v_ref, d, [(i * u + j, 0, None) for j in range(u)])
                return carry
            lax.fori_loop(0, d // u, classes, 0)
        elif n_blk == u:
            def one_class(c, carry, g=g, q_ref=q_ref, k_ref=k_ref, v_ref=v_ref, d=d):
                run_blocks(g, q_ref, k_ref, v_ref, d, [(c, nb, True if nb else None) for nb in range(u)])
                return carry
            lax.fori_loop(0, d, one_class, 0)
        else:
            def blocks(i, carry, g=g, q_ref=q_ref, k_ref=k_ref, v_ref=v_ref, d=d):
                run_blocks(g, q_ref, k_ref, v_ref, d,
                           [(0, i * u + j, (i > 0) if j == 0 else True) for j in range(u)])
                return carry
            assert d == 1 and n_blk % u == 0
            lax.fori_loop(0, n_blk // u, blocks, 0)


def _attn_prompt(proj, kv, bsz, seq_len):
    proj3 = proj.reshape(bsz, seq_len, proj.shape[1])
    kv3 = kv.reshape(bsz, seq_len, kv.shape[1])

    def col(base):
        return pl.BlockSpec((None, seq_len, LANES), lambda b, p: (b, 0, base + p))

    q_specs = [col(g * N_PAIRS) for g in range(N_GROUPS)]
    k_specs = [col(g * N_PAIRS) for g in range(N_GROUPS)]
    v_specs = [col((N_GROUPS + g) * N_PAIRS) for g in range(N_GROUPS)]
    out = pl.pallas_call(
        functools.partial(_attn_prompt_body, seq_len),
        out_shape=jax.ShapeDtypeStruct((bsz, seq_len, D_MODEL), F32),
        grid=(bsz, N_PAIRS),
        in_specs=q_specs + k_specs + v_specs,
        out_specs=pl.BlockSpec((None, seq_len, LANES), lambda b, p: (b, 0, p)),
        scratch_shapes=[pltpu.VMEM((seq_len, LANES), F32)] * 3,
        compiler_params=pltpu.CompilerParams(dimension_semantics=("arbitrary", "arbitrary"),
                                             vmem_limit_bytes=VMEM_LIMIT),
        name="attn_prompt",
    )(proj3, proj3, proj3, kv3, kv3, kv3, kv3, kv3, kv3)
    return out.reshape(bsz * seq_len, D_MODEL)


def _attn_sample_body(n_new, n_split, q_ref, kvn_ref, c0_ref, c1_ref, c2_ref, o_ref, m_s, l_s, acc_s):
    w = pl.program_id(1)
    rows = n_new * N_HEADS
    rid = lax.broadcasted_iota(jnp.int32, (rows, D_MODEL), 0)
    cid = lax.broadcasted_iota(jnp.int32, (rows, D_MODEL), 1)
    own_head = (rid % N_HEADS) == (cid // HEAD_DIM)
    jrow = lax.broadcasted_iota(jnp.int32, (rows, 1), 0) // N_HEADS

    def q_rows(g):
        qg = q_ref[:, g * D_MODEL:(g + 1) * D_MODEL] * ATT_SCALE
        full = jnp.concatenate([jnp.broadcast_to(qg[j:j + 1], (N_HEADS, D_MODEL)) for j in range(n_new)], axis=0)
        return jnp.where(own_head, full, 0.0)

    def absorb_cache(state, g, qb, kt, vt, first_row):
        d = DILATIONS[g]
        s = _dot(qb.astype(BF16), kt.astype(BF16))
        r = first_row + lax.broadcasted_iota(jnp.int32, (1, s.shape[1]), 1)
        s = jnp.where((r % d == jrow % d) & (r >= jrow + d), s, NEG_INF)
        m_c = jnp.max(s, axis=-1, keepdims=True)
        if state is None:
            m_n, alpha = m_c, None
        else:
            m_n = jnp.maximum(state[0], m_c)
            alpha = jnp.exp(state[0] - m_n)
        e = jnp.exp(s - m_n)
        l_c = jnp.sum(e, axis=-1, keepdims=True)
        a_c = _dot_nt(e.astype(BF16), vt.astype(BF16))
        if state is None:
            return m_n, l_c, a_c
        return m_n, alpha * state[1] + l_c, alpha * state[2] + a_c

    def absorb_new(state, g, qb):
        d = DILATIONS[g]
        k_new = kvn_ref[:, g * D_MODEL:(g + 1) * D_MODEL]
        v_new = kvn_ref[:, (N_GROUPS + g) * D_MODEL:(N_GROUPS + g + 1) * D_MODEL]
        s = []
        for i in range(n_new):
            ok = (jrow >= i) & ((jrow - i) % d == 0)
            s.append(jnp.where(ok, jnp.sum(qb * k_new[i:i + 1], axis=-1, keepdims=True), NEG_INF))
        m_n = functools.reduce(jnp.maximum, s, state[0])
        alpha = jnp.exp(state[0] - m_n)
        l_n, a_n = alpha * state[1], alpha * state[2]
        for i in range(n_new):
            e = jnp.exp(s[i] - m_n)
            l_n = l_n + e
            a_n = a_n + e * v_new[i:i + 1]
        return m_n, l_n, a_n

    wide = c2_ref.shape[2]

    def last_group_chunk(state):
        st = absorb_cache(state, N_GROUPS - 1, q_rows(N_GROUPS - 1), c2_ref[0], c2_ref[1], w * wide)
        m_s[...], l_s[...], acc_s[...] = st

    @pl.when(w == 0)
    def _():
        st = None
        for g, c_ref in ((0, c0_ref), (1, c1_ref)):
            qb = q_rows(g)
            st = absorb_new(absorb_cache(st, g, qb, c_ref[0], c_ref[1], 0), g, qb)
        st = absorb_new(st, N_GROUPS - 1, q_rows(N_GROUPS - 1))
        last_group_chunk(st)

    @pl.when(w > 0)
    def _():
        last_group_chunk((m_s[...], l_s[...], acc_s[...]))

    @pl.when(w == n_split - 1)
    def _():
        o = jnp.where(own_head, acc_s[...] / l_s[...], 0.0)
        o_ref[...] = jnp.sum(o.reshape(n_new, N_HEADS, D_MODEL), axis=1)


def _attn_sample(proj, kv, caches_t, bsz, n_new):
    proj3 = proj.reshape(bsz, n_new, proj.shape[1])
    kv3 = kv.reshape(bsz, n_new, kv.shape[1])
    n_split = ATTN_S_SPLIT
    specs = []
    for g in range(N_GROUPS):
        d = DILATIONS[g]
        assert caches_t[g].shape[3] == WINDOWS[g] and (d == 1 or n_new <= d)
        if g < N_GROUPS - 1:
            specs.append(pl.BlockSpec((None, 2, D_MODEL, WINDOWS[g]), lambda b, w: (b, 0, 0, 0)))
        else:
            specs.append(pl.BlockSpec((None, 2, D_MODEL, WINDOWS[g] // n_split), lambda b, w: (b, 0, 0, w)))
    rows = n_new * N_HEADS
    out = pl.pallas_call(
        functools.partial(_attn_sample_body, n_new, n_split),
        out_shape=jax.ShapeDtypeStruct((bsz, n_new, D_MODEL), F32),
        grid=(bsz, n_split),
        in_specs=[pl.BlockSpec((None, n_new, proj.shape[1]), lambda b, w: (b, 0, 0)),
                  pl.BlockSpec((None, n_new, kv.shape[1]), lambda b, w: (b, 0, 0))] + specs,
        out_specs=pl.BlockSpec((None, n_new, D_MODEL), lambda b, w: (b, 0, 0)),
        scratch_shapes=[pltpu.VMEM((rows, 1), F32), pltpu.VMEM((rows, 1), F32), pltpu.VMEM((rows, D_MODEL), F32)],
        compiler_params=pltpu.CompilerParams(dimension_semantics=("arbitrary", "arbitrary"),
                                             vmem_limit_bytes=VMEM_LIMIT),
        name="attn_sample",
    )(proj3, kv3, *caches_t)
    return out.reshape(bsz * n_new, D_MODEL)


def _b_post_body(final, *refs):
    if final:
        o_ref, z_ref, x_ref, wo_ref, fl_ref, out_ref = refs
    else:
        o_ref, z_ref, x_ref, wo_ref, out_ref = refs
    z = z_ref[...]
    gated = o_ref[...] * (z * _sigmoid(z))
    h = x_ref[...] + _dot(gated.astype(BF16), wo_ref[...])
    out_ref[...] = _rms(h, fl_ref[...]) if final else h


def _b_post(o, proj, x, w_out, final_ln):
    n = x.shape[0]
    tm = min(512, n)
    final = final_ln is not None
    row = pl.BlockSpec((tm, D_MODEL), lambda i: (i, 0))
    specs = [row, pl.BlockSpec((tm, D_MODEL), lambda i: (i, N_GROUPS)), row,
             pl.BlockSpec((D_MODEL, D_MODEL), lambda i: (0, 0))]
    args = [o, proj, x, w_out]
    if final:
        specs.append(pl.BlockSpec((1, D_MODEL), lambda i: (0, 0)))
        args.append(final_ln)
    return pl.pallas_call(
        functools.partial(_b_post_body, final),
        out_shape=jax.ShapeDtypeStruct((n, D_MODEL), F32),
        grid=(n // tm,),
        in_specs=specs,
        out_specs=row,
        compiler_params=pltpu.CompilerParams(dimension_semantics=("arbitrary",), vmem_limit_bytes=VMEM_LIMIT),
        name="b_post",
    )(*args)


def _kv_tail_t_body(x_ref, o_ref):
    o_ref[...] = x_ref[...].T


def _kv_tail_t(kv, bsz, seq_len, g, wc):
    tk = min(512, wc)
    first = (seq_len - wc) // tk
    kv3 = kv.reshape(bsz, seq_len, kv.shape[1])
    return pl.pallas_call(
        _kv_tail_t_body,
        out_shape=jax.ShapeDtypeStruct((bsz, 2, D_MODEL, wc), F32),
        grid=(bsz, 2, wc // tk),
        in_specs=[pl.BlockSpec((None, tk, D_MODEL), lambda b, kv_i, i: (b, first + i, kv_i * N_GROUPS + g))],
        out_specs=pl.BlockSpec((None, None, D_MODEL, tk), lambda b, kv_i, i: (b, kv_i, 0, i)),
        compiler_params=pltpu.CompilerParams(dimension_semantics=("arbitrary",) * 3, vmem_limit_bytes=VMEM_LIMIT),
        name="kv_tail_t",
    )(kv3)


def _rope_tables(pos):
    half = HEAD_DIM // 2
    inv = ROPE_THETA ** (-jnp.arange(half, dtype=F32) / half)
    ang = pos.astype(F32)[:, None] * inv[None, :]
    cos, sin = jnp.cos(ang), jnp.sin(ang)
    return jnp.concatenate([cos, cos, cos, cos], axis=1), jnp.concatenate([-sin, sin, -sin, sin], axis=1)


def _rwkv_stack(x, shift_init, wkv_init, n_seq, seq_len, layers):
    shifts, states = [], []
    v_first = None
    if seq_len % WKV_CHUNK == 0:
        chunk, rows_per_step, pad = WKV_CHUNK, 2, 0
    else:
        chunk, rows_per_step = WKV_SMALL_CHUNK, 2
        pad = (-seq_len) % chunk
    for l, w in enumerate(layers):
        (r, lw, k, v, kk, b, z), sh = _a_pre(x, shift_init[l], v_first, w, n_seq, seq_len)
        if l == 0:
            v_first = v
        seqs = [t.reshape(n_seq, seq_len, D_MODEL) for t in (r, lw, k, v, kk, b)]
        if pad:
            seqs = [jnp.pad(t, ((0, 0), (0, pad), (0, 0))) for t in seqs]
        y, st = _wkv(seqs, None if wkv_init is None else wkv_init[l], chunk, rows_per_step)
        y = y[:, :seq_len].reshape(n_seq * seq_len, D_MODEL)
        x = _a_post(y, r, k, v, z, x, w)
        shifts.append(sh)
        states.append(st)
    return x, jnp.stack(shifts), jnp.stack(states)


def kernel(x_prompt, x_sample, state_wkv, state_shift, cache_kv_g0, cache_kv_g1, cache_kv_g2, a_ln, a_mu, a_w_in, a_w0, a_w_lora_a, a_w_lora_b, a_a0, a_a_lora_a, a_a_lora_b, a_v0, a_v_lora_a, a_v_lora_b, a_k_k, a_k_a, a_r_k, a_gn_w, a_gn_b, a_w_out, kv_ln, w_kv, b_ln, b_w_in, b_w_out, final_ln):
    bp, s, _ = x_prompt.shape
    bs, t, _ = x_sample.shape
    n_a = a_ln.shape[0]
    n_b = b_ln.shape[0]

    def row(vec):
        return vec.reshape(1, D_MODEL)

    layers = []
    for l in range(n_a):
        w = dict(ln=row(a_ln[l]), mu=a_mu[l], w_in=a_w_in[l].astype(BF16), w0=row(a_w0[l]),
                 w_la=a_w_lora_a[l].astype(BF16), w_lb=a_w_lora_b[l].astype(BF16), a0=row(a_a0[l]),
                 a_la=a_a_lora_a[l].astype(BF16), a_lb=a_a_lora_b[l].astype(BF16),
                 k_k=row(a_k_k[l]), k_a=row(a_k_a[l]), r_k=row(a_r_k[l]), gn_w=row(a_gn_w[l]),
                 gn_b=row(a_gn_b[l]), w_out=a_w_out[l].astype(BF16))
        if l > 0:
            w.update(v0=row(a_v0[l - 1]), v_la=a_v_lora_a[l - 1].astype(BF16), v_lb=a_v_lora_b[l - 1].astype(BF16))
        layers.append(w)

    xp = x_prompt.reshape(bp * s, D_MODEL)
    xs = x_sample.reshape(bs * t, D_MODEL)
    zero_shift = jnp.zeros((n_a, bp, D_MODEL), F32)
    hp, shift_p, wkv_p = _rwkv_stack(xp, zero_shift, None, bp, s, layers)
    hs, shift_s, wkv_s = _rwkv_stack(xs, state_shift, state_wkv, bs, t, layers)

    cos_p, sin_p = _rope_tables(jnp.arange(s, dtype=jnp.int32))
    cos_s, sin_s = _rope_tables(jnp.tile(PAST_LEN + jnp.arange(t, dtype=jnp.int32), bs))
    w_kv_b = w_kv.astype(BF16)
    kv_ln_r = row(kv_ln)
    kvp = _proj(hp, kv_ln_r, w_kv_b, cos_p, sin_p, N_GROUPS)
    kvs = _proj(hs, kv_ln_r, w_kv_b, cos_s, sin_s, N_GROUPS)

    caches = tuple(jnp.transpose(c, (0, 2, 3, 4, 1)).reshape(bs, 2, D_MODEL, c.shape[1])
                   for c in (cache_kv_g0, cache_kv_g1, cache_kv_g2))
    for l in range(n_b):
        w_in = b_w_in[l].astype(BF16)
        w_out = b_w_out[l].astype(BF16)
        fl = row(final_ln) if l == n_b - 1 else None
        pj = _proj(hp, row(b_ln[l]), w_in, cos_p, sin_p, N_GROUPS)
        hp = _b_post(_attn_prompt(pj, kvp, bp, s), pj, hp, w_out, fl)
        pj = _proj(hs, row(b_ln[l]), w_in, cos_s, sin_s, N_GROUPS)
        hs = _b_post(_attn_sample(pj, kvs, caches, bs, t), pj, hs, w_out, fl)

    y_prompt = hp.reshape(bp, s, D_MODEL)
    y_sample = hs.reshape(bs, t, D_MODEL)

    kvs3 = kvs.reshape(bs, t, 2, N_GROUPS, N_HEADS, HEAD_DIM)
    kv_out = []
    for g in range(N_GROUPS):
        wc = min(WINDOWS[g], s)
        rows_minor = _kv_tail_t(kvp, bp, s, g, wc).reshape(bp, 2, N_HEADS, HEAD_DIM, wc)
        kv_out.append(jnp.transpose(rows_minor, (0, 4, 1, 2, 3)))
        kv_out.append(kvs3[:, :, :, g])
    return (y_prompt, y_sample, wkv_p, wkv_s, shift_p, shift_s, *kv_out)
```

```python
import functools

import jax
import jax.numpy as jnp
from jax import lax
from jax.experimental import pallas as pl
from jax.experimental.pallas import tpu as pltpu

F32 = jnp.float32
BF16 = jnp.bfloat16

D_MODEL = 1024
HEAD_DIM = 64
N_HEADS = D_MODEL // HEAD_DIM
LANES = 128
N_PAIRS = D_MODEL // LANES
N_GROUPS = 3
WINDOWS = (128, 512, 2048)
DILATIONS = (1, 4, 16)
KEYS_PER_QUERY = 128
ATT_SCALE = HEAD_DIM ** -0.5
ROPE_THETA = 10000.0
NEG_INF = -1e30
GN_EPS = 64e-5
RMS_EPS = 1e-6
PAST_LEN = 2048
WKV_CHUNK = 64
WKV_SMALL_CHUNK = 8
ATTN_UNROLL = 4
ATTN_S_SPLIT = 2
COMPACT_DIL = 4
VMEM_LIMIT = 56 * 1024 * 1024

_HI = lax.Precision.HIGHEST


def _dot(a, b):
    return jnp.dot(a, b, preferred_element_type=F32)


def _dot_nt(a, b):
    return lax.dot_general(a, b, (((1,), (1,)), ((), ())), preferred_element_type=F32)


def _dot_tn(a, b):
    return lax.dot_general(a, b, (((0,), (0,)), ((), ())), preferred_element_type=F32)


def _rms(x, g):
    return x * lax.rsqrt(jnp.mean(x * x, axis=-1, keepdims=True) + RMS_EPS) * g


def _sigmoid(x):
    return 1.0 / (1.0 + jnp.exp(-x))


def _block_ones():
    r = lax.broadcasted_iota(jnp.int32, (LANES, LANES), 0) // HEAD_DIM
    c = lax.broadcasted_iota(jnp.int32, (LANES, LANES), 1) // HEAD_DIM
    return (r == c).astype(BF16)


def _head_sum(x, bo):
    outs = []
    for g in range(x.shape[1] // LANES):
        xg = x[:, g * LANES:(g + 1) * LANES]
        hi = xg.astype(BF16)
        lo = (xg - hi.astype(F32)).astype(BF16)
        outs.append(_dot(hi, bo) + _dot(lo, bo))
    return outs[0] if len(outs) == 1 else jnp.concatenate(outs, axis=1)


def _a_pre_body(has_vres, per_row_shift, seq_len, *refs):
    it = iter(refs)
    x_ref, sp_ref = next(it), next(it)
    vf_ref = next(it) if has_vres else None
    ln_ref, mu_ref, win_ref = next(it), next(it), next(it)
    w0_ref, wla_ref, wlb_ref = next(it), next(it), next(it)
    a0_ref, ala_ref, alb_ref = next(it), next(it), next(it)
    if has_vres:
        v0_ref, vla_ref, vlb_ref = next(it), next(it), next(it)
    kk_ref, ka_ref = next(it), next(it)
    r_o, lw_o, k_o, v_o, kk_o, b_o, z_o, sh_o = [next(it) for _ in range(8)]
    scr = next(it)

    tm = x_ref.shape[0]
    xn = _rms(x_ref[...], ln_ref[...])
    rolled = pltpu.roll(xn, 1, axis=0)
    row = lax.broadcasted_iota(jnp.int32, (tm, 1), 0)
    if per_row_shift:
        prev = jnp.where(row % seq_len == 0, sp_ref[...], rolled)
        for g in range(N_PAIRS):
            lanes = slice(g * LANES, (g + 1) * LANES)
            scr[g] = xn[:, lanes]
            sh_o[:, lanes] = scr[g, pl.ds(seq_len - 1, tm // seq_len, stride=seq_len), :]
    else:
        first = jnp.where(pl.program_id(1) == 0, sp_ref[...], scr[0:1, :])
        prev = jnp.where(row == 0, first, rolled)
        scr[0:1, :] = xn[tm - 1:tm, :]
        sh_o[...] = xn[tm - 1:tm, :]

    dx = prev - xn
    mu = mu_ref[...]

    def mixed(p):
        return xn + mu[p:p + 1, :] * dx

    xv = mixed(2)
    r = _dot(mixed(0).astype(BF16), win_ref[0])
    k = _dot(mixed(1).astype(BF16), win_ref[1])
    v = _dot(xv.astype(BF16), win_ref[2])
    z_o[...] = _dot(mixed(3).astype(BF16), win_ref[3])

    def lora(xm, la_ref, lb_ref, act):
        hdn = _dot(xm.astype(BF16), la_ref[...])
        if act:
            hdn = jnp.tanh(hdn)
        return _dot(hdn.astype(BF16), lb_ref[...])

    wl = w0_ref[...] + lora(mixed(4), wla_ref, wlb_ref, True)
    sp = jnp.maximum(-wl, 0.0) + jnp.log1p(jnp.exp(-jnp.abs(wl)))
    lw_o[...] = -jnp.exp(-sp - 0.5)
    a = _sigmoid(a0_ref[...] + lora(mixed(5), ala_ref, alb_ref, False))
    if has_vres:
        v = v + (vf_ref[...] - v) * _sigmoid(v0_ref[...] + lora(xv, vla_ref, vlb_ref, False))

    bo = _block_ones()
    kkp = k * kk_ref[...]
    kk = kkp * lax.rsqrt(jnp.maximum(_head_sum(kkp * kkp, bo), 1e-24))
    r_o[...] = r
    k_o[...] = k * (1.0 + (a - 1.0) * ka_ref[...])
    v_o[...] = v
    kk_o[...] = kk
    b_o[...] = kk * a


def _a_pre(x, shift_prev, v_first, w, n_seq, seq_len):
    n = x.shape[0]
    has_vres = v_first is not None
    per_row_shift = seq_len < 256
    if per_row_shift:
        tm = n
        grid = (1, 1)
        sp = jnp.repeat(shift_prev, seq_len, axis=0)
        sp_spec = pl.BlockSpec((tm, D_MODEL), lambda b, i: (0, 0))
        sh_shape = jax.ShapeDtypeStruct((n_seq, D_MODEL), F32)
        sh_spec = pl.BlockSpec((n_seq, D_MODEL), lambda b, i: (0, 0))
        scr = pltpu.VMEM((N_PAIRS, tm, LANES), F32)
        tiles = 1
    else:
        tm = 256
        tiles = seq_len // tm
        grid = (n_seq, tiles)
        sp = shift_prev.reshape(n_seq, 1, D_MODEL)
        sp_spec = pl.BlockSpec((None, 1, D_MODEL), lambda b, i: (b, 0, 0))
        sh_shape = jax.ShapeDtypeStruct((n_seq, 1, D_MODEL), F32)
        sh_spec = pl.BlockSpec((None, 1, D_MODEL), lambda b, i: (b, 0, 0))
        scr = pltpu.VMEM((8, D_MODEL), F32)

    row_spec = pl.BlockSpec((tm, D_MODEL), lambda b, i: (b * tiles + i, 0))

    def const(shape):
        return pl.BlockSpec(shape, lambda b, i: (0,) * len(shape))

    vec = const((1, D_MODEL))
    args = [x, sp]
    specs = [row_spec, sp_spec]
    if has_vres:
        args.append(v_first)
        specs.append(row_spec)
    args += [w['ln'], w['mu'], w['w_in'], w['w0'], w['w_la'], w['w_lb'], w['a0'], w['a_la'], w['a_lb']]
    specs += [vec, const((6, D_MODEL)), const((4, D_MODEL, D_MODEL)), vec, const(w['w_la'].shape),
              const(w['w_lb'].shape), vec, const(w['a_la'].shape), const(w['a_lb'].shape)]
    if has_vres:
        args += [w['v0'], w['v_la'], w['v_lb']]
        specs += [vec, const(w['v_la'].shape), const(w['v_lb'].shape)]
    args += [w['k_k'], w['k_a']]
    specs += [vec, vec]

    full = jax.ShapeDtypeStruct((n, D_MODEL), F32)
    outs = pl.pallas_call(
        functools.partial(_a_pre_body, has_vres, per_row_shift, seq_len),
        out_shape=[full] * 7 + [sh_shape],
        grid=grid,
        in_specs=specs,
        out_specs=[row_spec] * 7 + [sh_spec],
        scratch_shapes=[scr],
        compiler_params=pltpu.CompilerParams(dimension_semantics=("arbitrary", "arbitrary"),
                                             vmem_limit_bytes=VMEM_LIMIT),
        name="a_pre",
    )(*args)
    return outs[:7], outs[7].reshape(n_seq, D_MODEL)


def _wkv_body(zero_init, L, *refs):
    it = iter(refs)
    r_ref, lw_ref, k_ref, v_ref, kk_ref, b_ref = [next(it) for _ in range(6)]
    st_ref = None if zero_init else next(it)
    y_ref, so_ref = next(it), next(it)
    sbd = next(it)

    n_b = r_ref.shape[0]
    c = pl.program_id(1)
    nc = pl.num_programs(1)
    half = HEAD_DIM
    chains = [(bi, p) for bi in range(n_b) for p in range(N_PAIRS)]

    @pl.when(c == 0)
    def _():
        if zero_init:
            sbd[...] = jnp.zeros(sbd.shape, F32)
        else:
            zeros = jnp.zeros((half, half), F32)
            for bi, p in chains:
                top = jnp.concatenate([st_ref[bi, 2 * p], zeros], axis=1)
                bot = jnp.concatenate([zeros, st_ref[bi, 2 * p + 1]], axis=1)
                sbd[bi, p] = jnp.concatenate([top, bot], axis=0)

    rr = lax.broadcasted_iota(jnp.int32, (L, L), 0)
    cc = lax.broadcasted_iota(jnp.int32, (L, L), 1)
    tri = (rr >= cc).astype(F32)
    lane = lax.broadcasted_iota(jnp.int32, (1, LANES), 1)
    h0 = lane < half
    r2 = lax.broadcasted_iota(jnp.int32, (2 * L, 2 * L), 0)
    c2 = lax.broadcasted_iota(jnp.int32, (2 * L, 2 * L), 1)
    strict = r2 > c2
    incl = r2 >= c2
    eye = (r2 == c2).astype(F32)
    blk = min(16, L)
    diag_blk = (r2 // blk) == (c2 // blk)

    def split(x):
        return jnp.concatenate([jnp.where(h0, x, 0.0), jnp.where(h0, 0.0, x)], axis=0).astype(BF16)

    def mm(a, b):
        return _dot(a.astype(BF16), b.astype(BF16))

    def neumann(a, n_terms):
        x = {ch: eye - a[ch] for ch in chains}
        pw, k = a, 2
        while k < n_terms:
            pw = {ch: mm(pw[ch], pw[ch]) for ch in chains}
            x = {ch: mm(x[ch], eye + pw[ch]) for ch in chains}
            k *= 2
        return x

    lhs4, bts, kls, v2s, bks, e_gls = {}, {}, {}, {}, {}, {}
    for bi in range(n_b):
        lw = lw_ref[bi]
        g_incl = jnp.dot(tri, lw, precision=_HI, preferred_element_type=F32)
        e_g = jnp.exp(g_incl)
        e_ng = jnp.exp(-g_incl)
        e_gl = e_g[L - 1:L, :]
        kt = kk_ref[bi] * jnp.exp(g_incl - lw)
        rt = r_ref[bi] * e_g
        bt = b_ref[bi] * e_ng
        kl = k_ref[bi] * e_ng
        bh = bt * e_gl
        kh = kl * e_gl
        vv = v_ref[bi]
        for p in range(N_PAIRS):
            sl = slice(p * LANES, (p + 1) * LANES)
            lhs4[bi, p] = jnp.concatenate([split(kt[:, sl]), split(rt[:, sl])], axis=0)
            bts[bi, p] = split(bt[:, sl])
            kls[bi, p] = split(kl[:, sl])
            v2s[bi, p] = split(vv[:, sl])
            bks[bi, p] = jnp.concatenate([split(bh[:, sl]), split(kh[:, sl])], axis=0)
            e_gls[bi, p] = e_gl[:, sl]

    wide_ok = (2 * L) % LANES == 0
    if wide_ok:
        aa = {ch: _dot_nt(lhs4[ch], jnp.concatenate([bts[ch], kls[ch]], axis=0)) for ch in chains}
        ab = {ch: aa[ch][:, 0:2 * L] for ch in chains}
        ak = {ch: aa[ch][:, 2 * L:4 * L] for ch in chains}
    else:
        ab = {ch: _dot_nt(lhs4[ch], bts[ch]) for ch in chains}
        ak = {ch: _dot_nt(lhs4[ch], kls[ch]) for ch in chains}
    a_kb = {ch: jnp.where(strict, ab[ch][0:2 * L], 0.0) for ch in chains}
    a_rb = {ch: jnp.where(incl, ab[ch][2 * L:4 * L], 0.0).astype(BF16) for ch in chains}
    a_kk = {ch: jnp.where(strict, ak[ch][0:2 * L], 0.0).astype(BF16) for ch in chains}
    a_rk = {ch: jnp.where(incl, ak[ch][2 * L:4 * L], 0.0).astype(BF16) for ch in chains}
    if L > blk:
        a_d = {ch: jnp.where(diag_blk, a_kb[ch], 0.0) for ch in chains}
        x_d = neumann(a_d, blk)
        nn = {ch: mm(x_d[ch], a_kb[ch] - a_d[ch]) for ch in chains}
        x_n = neumann(nn, L // blk)
        x_f = {ch: mm(x_n[ch], x_d[ch]).astype(BF16) for ch in chains}
    else:
        x_f = {ch: v.astype(BF16) for ch, v in neumann(a_kb, L).items()}

    s_old = {ch: sbd[ch[0], ch[1]] for ch in chains}
    pp = {ch: _dot_nt(lhs4[ch], s_old[ch].astype(BF16)) for ch in chains}
    rhs = {ch: -(pp[ch][0:2 * L] + _dot(a_kk[ch], v2s[ch])) for ch in chains}
    u2 = {ch: _dot(x_f[ch], rhs[ch].astype(BF16)).astype(BF16) for ch in chains}
    for ch in chains:
        bi, p = ch
        uv = jnp.concatenate([u2[ch], v2s[ch]], axis=0)
        if wide_ok:
            ys = pp[ch][2 * L:4 * L] + _dot(jnp.concatenate([a_rb[ch], a_rk[ch]], axis=1), uv)
        else:
            ys = pp[ch][2 * L:4 * L] + _dot(a_rb[ch], u2[ch]) + _dot(a_rk[ch], v2s[ch])
        y_ref[bi, :, p * LANES:(p + 1) * LANES] = ys[0:L] + ys[L:2 * L]
        sbd[bi, p] = s_old[ch] * e_gls[ch] + _dot_tn(uv, bks[ch])

    @pl.when(c == nc - 1)
    def _():
        for bi, p in chains:
            s_p = sbd[bi, p]
            so_ref[bi, 2 * p] = s_p[0:half, 0:half]
            so_ref[bi, 2 * p + 1] = pltpu.roll(s_p[half:2 * half, :], half, axis=1)[:, 0:half]


def _wkv(seqs, state0, chunk, rows_per_step):
    bsz, t, _ = seqs[0].shape
    zero_init = state0 is None
    nb = rows_per_step
    tok = pl.BlockSpec((nb, chunk, D_MODEL), lambda b, c: (b, c, 0))
    st = pl.BlockSpec((nb, N_HEADS, HEAD_DIM, HEAD_DIM), lambda b, c: (b, 0, 0, 0))
    args = list(seqs) + ([] if zero_init else [state0])
    specs = [tok] * 6 + ([] if zero_init else [st])
    return pl.pallas_call(
        functools.partial(_wkv_body, zero_init, chunk),
        out_shape=[jax.ShapeDtypeStruct((bsz, t, D_MODEL), F32),
                   jax.ShapeDtypeStruct((bsz, N_HEADS, HEAD_DIM, HEAD_DIM), F32)],
        grid=(bsz // nb, t // chunk),
        in_specs=specs,
        out_specs=[tok, st],
        scratch_shapes=[pltpu.VMEM((nb, N_PAIRS, LANES, LANES), F32)],
        compiler_params=pltpu.CompilerParams(dimension_semantics=("arbitrary", "arbitrary"),
                                             vmem_limit_bytes=VMEM_LIMIT),
        name="wkv",
    )(*args)


def _a_post_body(y_ref, r_ref, k_ref, v_ref, z_ref, x_ref, rk_ref, gw_ref, gb_ref, wo_ref, o_ref):
    bo = _block_ones()
    y = y_ref[...]
    inv_c = 1.0 / HEAD_DIM
    yc = y - _head_sum(y, bo) * inv_c
    var = _head_sum(yc * yc, bo) * inv_c
    yn = yc * lax.rsqrt(var + GN_EPS) * gw_ref[...] + gb_ref[...]
    yn = yn + _head_sum(r_ref[...] * k_ref[...] * rk_ref[...], bo) * v_ref[...]
    z = z_ref[...]
    gated = yn * (z * _sigmoid(z))
    o_ref[...] = x_ref[...] + _dot(gated.astype(BF16), wo_ref[...])


def _a_post(y, r, k, v, z, x, w):
    n = x.shape[0]
    tm = 512
    row = pl.BlockSpec((tm, D_MODEL), lambda i: (i, 0))
    vec = pl.BlockSpec((1, D_MODEL), lambda i: (0, 0))
    return pl.pallas_call(
        _a_post_body,
        out_shape=jax.ShapeDtypeStruct((n, D_MODEL), F32),
        grid=(n // tm,),
        in_specs=[row] * 6 + [vec] * 3 + [pl.BlockSpec((D_MODEL, D_MODEL), lambda i: (0, 0))],
        out_specs=row,
        compiler_params=pltpu.CompilerParams(dimension_semantics=("arbitrary",), vmem_limit_bytes=VMEM_LIMIT),
        name="a_post",
    )(y, r, k, v, z, x, w['r_k'], w['gn_w'], w['gn_b'], w['w_out'])


def _rope_tile(t, cos_t, sin_t):
    lane = lax.broadcasted_iota(jnp.int32, (1, LANES), 1)
    first_half = (lane % HEAD_DIM) < (HEAD_DIM // 2)
    outs = []
    for g in range(t.shape[1] // LANES):
        xg = t[:, g * LANES:(g + 1) * LANES]
        partner = jnp.where(first_half, pltpu.roll(xg, LANES - HEAD_DIM // 2, axis=1),
                            pltpu.roll(xg, HEAD_DIM // 2, axis=1))
        outs.append(xg * cos_t + partner * sin_t)
    return jnp.concatenate(outs, axis=1)


def _proj_body(n_rope, x_ref, ln_ref, w_ref, cos_ref, sin_ref, o_ref):
    xn = _rms(x_ref[...], ln_ref[...]).astype(BF16)
    for j in range(o_ref.shape[1] // D_MODEL):
        cols = slice(j * D_MODEL, (j + 1) * D_MODEL)
        t = _dot(xn, w_ref[:, cols])
        o_ref[:, cols] = _rope_tile(t, cos_ref[...], sin_ref[...]) if j < n_rope else t


def _proj(x, ln, w, cos_t, sin_t, n_rope):
    n = x.shape[0]
    n_out = w.shape[1]
    tm = min(256, n)
    n_tab = cos_t.shape[0] // tm
    tab = pl.BlockSpec((tm, LANES), lambda i: (i % n_tab, 0))
    return pl.pallas_call(
        functools.partial(_proj_body, n_rope),
        out_shape=jax.ShapeDtypeStruct((n, n_out), F32),
        grid=(n // tm,),
        in_specs=[pl.BlockSpec((tm, D_MODEL), lambda i: (i, 0)),
                  pl.BlockSpec((1, D_MODEL), lambda i: (0, 0)),
                  pl.BlockSpec((D_MODEL, n_out), lambda i: (0, 0)),
                  tab, tab],
        out_specs=pl.BlockSpec((tm, n_out), lambda i: (i, 0)),
        compiler_params=pltpu.CompilerParams(dimension_semantics=("arbitrary",), vmem_limit_bytes=VMEM_LIMIT),
        name="proj",
    )(x, ln, w, cos_t, sin_t)


def _attn_prompt_body(seq_len, q0, q1, q2, k0, k1, k2, v0, v1, v2, o_ref, m_s, l_s, a_s):
    nq = KEYS_PER_QUERY
    lane = lax.broadcasted_iota(jnp.int32, (1, LANES), 1)
    h0 = lane < HEAD_DIM
    qi = lax.broadcasted_iota(jnp.int32, (nq, nq), 0)
    ki = lax.broadcasted_iota(jnp.int32, (nq, nq), 1)
    cur_mask = ki <= qi
    prev_mask = ki > qi

    heads = (h0, jnp.logical_not(h0))

    def run_blocks(g, q_ref, k_ref, v_ref, d, blocks):
        rows, qs, kbs, vbs, masks = [], [], [], [], []
        for c, nb, prev in blocks:
            rw = pl.ds(c + d * nq * nb, nq, stride=d)
            kb = k_ref[rw, :].astype(BF16)
            vb = v_ref[rw, :].astype(BF16)
            mask = cur_mask
            if prev is not None:
                pb = nb - 1 if prev is True else jnp.maximum(nb - 1, 0)
                prow = pl.ds(c + d * nq * pb, nq, stride=d)
                kb = jnp.concatenate([k_ref[prow, :].astype(BF16), kb], axis=0)
                vb = jnp.concatenate([v_ref[prow, :].astype(BF16), vb], axis=0)
                pm = prev_mask if prev is True else jnp.logical_and(prev_mask, prev)
                mask = jnp.concatenate([pm, cur_mask], axis=1)
            rows.append(rw)
            qs.append(q_ref[rw, :] * ATT_SCALE)
            kbs.append(kb)
            vbs.append(vb)
            masks.append(mask)
        n = len(blocks)
        pairs = [(i, h) for i in range(n) for h in range(2)]
        s = {(i, h): _dot_nt(jnp.where(heads[h], qs[i], 0.0).astype(BF16), kbs[i]) for i, h in pairs}
        s = {ih: jnp.where(masks[ih[0]], s[ih], NEG_INF) for ih in pairs}
        mx = {ih: jnp.max(s[ih], axis=-1, keepdims=True) for ih in pairs}
        pr = {ih: jnp.exp(s[ih] - mx[ih]) for ih in pairs}
        ls = {ih: jnp.sum(pr[ih], axis=-1, keepdims=True) for ih in pairs}
        acc = {(i, h): _dot(pr[i, h].astype(BF16), vbs[i]) for i, h in pairs}
        for i in range(n):
            m_b = jnp.where(h0, mx[i, 0], mx[i, 1])
            l_b = jnp.where(h0, ls[i, 0], ls[i, 1])
            a_b = jnp.where(h0, acc[i, 0], acc[i, 1])
            if g > 0:
                m_o, l_o, a_o = m_s[rows[i], :], l_s[rows[i], :], a_s[rows[i], :]
                m_n = jnp.maximum(m_o, m_b)
                w_o = jnp.exp(m_o - m_n)
                w_b = jnp.exp(m_b - m_n)
                m_b = m_n
                l_b = w_o * l_o + w_b * l_b
                a_b = w_o * a_o + w_b * a_b
            if g == N_GROUPS - 1:
                o_ref[rows[i], :] = a_b / l_b
            else:
                m_s[rows[i], :] = m_b
                l_s[rows[i], :] = l_b
                a_s[rows[i], :] = a_b

    u = ATTN_UNROLL
    for g, (q_ref, k_ref, v_ref) in enumerate(((q0, k0, v0), (q1, k1, v1), (q2, k2, v2))):
        d = DILATIONS[g]
        n_blk = seq_len // (d * nq)
        if n_blk == 1:
            def classes(i, carry, g=g, q_ref=q_ref, k_ref=k_ref, v_ref=v_ref, d=d):
                run_blocks(g, q_ref, k_ref, v_ref, d, [(i * u + j, 0, None) for j in range(u)])
                return carry
            lax.fori_loop(0, d // u, classes, 0)
        elif n_blk == u:
            def one_class(c, carry, g=g, q_ref=q_ref, k_ref=k_ref, v_ref=v_ref, d=d):
                run_blocks(g, q_ref, k_ref, v_ref, d, [(c, nb, True if nb else None) for nb in range(u)])
                return carry
            lax.fori_loop(0, d, one_class, 0)
        else:
            def blocks(i, carry, g=g, q_ref=q_ref, k_ref=k_ref, v_ref=v_ref, d=d):
                run_blocks(g, q_ref, k_ref, v_ref, d,
                           [(0, i * u + j, (i > 0) if j == 0 else True) for j in range(u)])
                return carry
            assert d == 1 and n_blk % u == 0
            lax.fori_loop(0, n_blk // u, blocks, 0)


def _attn_prompt(proj, kv, bsz, seq_len):
    proj3 = proj.reshape(bsz, seq_len, proj.shape[1])
    kv3 = kv.reshape(bsz, seq_len, kv.shape[1])

    def col(base):
        return pl.BlockSpec((None, seq_len, LANES), lambda b, p: (b, 0, base + p))

    q_specs = [col(g * N_PAIRS) for g in range(N_GROUPS)]
    k_specs = [col(g * N_PAIRS) for g in range(N_GROUPS)]
    v_specs = [col((N_GROUPS + g) * N_PAIRS) for g in range(N_GROUPS)]
    out = pl.pallas_call(
        functools.partial(_attn_prompt_body, seq_len),
        out_shape=jax.ShapeDtypeStruct((bsz, seq_len, D_MODEL), F32),
        grid=(bsz, N_PAIRS),
        in_specs=q_specs + k_specs + v_specs,
        out_specs=pl.BlockSpec((None, seq_len, LANES), lambda b, p: (b, 0, p)),
        scratch_shapes=[pltpu.VMEM((seq_len, LANES), F32)] * 3,
        compiler_params=pltpu.CompilerParams(dimension_semantics=("arbitrary", "arbitrary"),
                                             vmem_limit_bytes=VMEM_LIMIT),
        name="attn_prompt",
    )(proj3, proj3, proj3, kv3, kv3, kv3, kv3, kv3, kv3)
    return out.reshape(bsz * seq_len, D_MODEL)


def _attn_sample_body(n_new, n_split, compacted, emit, *refs):
    if emit:
        q_ref, kvn_ref, c0_ref, c1_ref, c2_ref, o_ref, cc_ref, m_s, l_s, acc_s = refs
    else:
        q_ref, kvn_ref, c0_ref, c1_ref, c2_ref, o_ref, m_s, l_s, acc_s = refs
    w = pl.program_id(1)
    rows = n_new * N_HEADS
    rid = lax.broadcasted_iota(jnp.int32, (rows, D_MODEL), 0)
    cid = lax.broadcasted_iota(jnp.int32, (rows, D_MODEL), 1)
    own_head = (rid % N_HEADS) == (cid // HEAD_DIM)
    jrow = lax.broadcasted_iota(jnp.int32, (rows, 1), 0) // N_HEADS

    def q_rows(g):
        qg = q_ref[:, g * D_MODEL:(g + 1) * D_MODEL] * ATT_SCALE
        full = jnp.concatenate([jnp.broadcast_to(qg[j:j + 1], (N_HEADS, D_MODEL)) for j in range(n_new)], axis=0)
        return jnp.where(own_head, full, 0.0)

    def absorb_cache(state, d, qb, kt, vt, first_row):
        s = _dot(qb.astype(BF16), kt)
        r = first_row + lax.broadcasted_iota(jnp.int32, (1, s.shape[1]), 1)
        s = jnp.where((r % d == jrow % d) & (r >= jrow + d), s, NEG_INF)
        m_c = jnp.max(s, axis=-1, keepdims=True)
        if state is None:
            m_n, alpha = m_c, None
        else:
            m_n = jnp.maximum(state[0], m_c)
            alpha = jnp.exp(state[0] - m_n)
        e = jnp.exp(s - m_n)
        l_c = jnp.sum(e, axis=-1, keepdims=True)
        a_c = _dot_nt(e.astype(BF16), vt)
        if state is None:
            return m_n, l_c, a_c
        return m_n, alpha * state[1] + l_c, alpha * state[2] + a_c

    def absorb_new(state, g, qb):
        d = DILATIONS[g]
        k_new = kvn_ref[:, g * D_MODEL:(g + 1) * D_MODEL]
        v_new = kvn_ref[:, (N_GROUPS + g) * D_MODEL:(N_GROUPS + g + 1) * D_MODEL]
        s = []
        for i in range(n_new):
            ok = (jrow >= i) & ((jrow - i) % d == 0)
            s.append(jnp.where(ok, jnp.sum(qb * k_new[i:i + 1], axis=-1, keepdims=True), NEG_INF))
        m_n = functools.reduce(jnp.maximum, s, state[0])
        alpha = jnp.exp(state[0] - m_n)
        l_n, a_n = alpha * state[1], alpha * state[2]
        for i in range(n_new):
            e = jnp.exp(s[i] - m_n)
            l_n = l_n + e
            a_n = a_n + e * v_new[i:i + 1]
        return m_n, l_n, a_n

    @pl.when(w == 0)
    def _():
        st = None
        for g, c_ref in ((0, c0_ref), (1, c1_ref)):
            qb = q_rows(g)
            st = absorb_cache(st, DILATIONS[g], qb, c_ref[0].astype(BF16), c_ref[1].astype(BF16), 0)
            st = absorb_new(st, g, qb)
        m_s[...], l_s[...], acc_s[...] = absorb_new(st, N_GROUPS - 1, q_rows(N_GROUPS - 1))

    wide = c2_ref.shape[2]
    kt, vt = c2_ref[0].astype(BF16), c2_ref[1].astype(BF16)
    lane_dil = COMPACT_DIL if compacted else DILATIONS[N_GROUPS - 1]
    m_s[...], l_s[...], acc_s[...] = absorb_cache((m_s[...], l_s[...], acc_s[...]), lane_dil,
                                                  q_rows(N_GROUPS - 1), kt, vt, w * wide)
    if emit:
        slab = LANES * COMPACT_DIL
        rr = lax.broadcasted_iota(jnp.int32, (slab, LANES), 0)
        cc = lax.broadcasted_iota(jnp.int32, (slab, LANES), 1)
        sel = (rr == DILATIONS[N_GROUPS - 1] * (cc // COMPACT_DIL) + cc % COMPACT_DIL).astype(BF16)
        for i, x in enumerate((kt, vt)):
            for k in range(wide // slab):
                cc_ref[i, :, k * LANES:(k + 1) * LANES] = _dot(x[:, k * slab:(k + 1) * slab], sel).astype(BF16)

    @pl.when(w == n_split - 1)
    def _():
        o = jnp.where(own_head, acc_s[...] / l_s[...], 0.0)
        o_ref[...] = jnp.sum(o.reshape(n_new, N_HEADS, D_MODEL), axis=1)


def _attn_sample(proj, kv, caches_t, bsz, n_new, emit):
    proj3 = proj.reshape(bsz, n_new, proj.shape[1])
    kv3 = kv.reshape(bsz, n_new, kv.shape[1])
    last = N_GROUPS - 1
    w_last = caches_t[last].shape[3]
    compacted = w_last != WINDOWS[last]
    assert n_new <= COMPACT_DIL and not (compacted and emit)
    assert w_last == (WINDOWS[last] // DILATIONS[last] * COMPACT_DIL if compacted else WINDOWS[last])
    n_split = 1 if compacted else ATTN_S_SPLIT
    specs = [pl.BlockSpec((None, 2, D_MODEL, WINDOWS[g]), lambda b, w: (b, 0, 0, 0)) for g in range(last)]
    specs.append(pl.BlockSpec((None, 2, D_MODEL, w_last // n_split), lambda b, w: (b, 0, 0, w)))
    for g in range(last):
        assert caches_t[g].shape[3] == WINDOWS[g]
    rows = n_new * N_HEADS
    out_shape = [jax.ShapeDtypeStruct((bsz, n_new, D_MODEL), F32)]
    out_specs = [pl.BlockSpec((None, n_new, D_MODEL), lambda b, w: (b, 0, 0))]
    if emit:
        w_c = w_last // DILATIONS[last] * COMPACT_DIL
        out_shape.append(jax.ShapeDtypeStruct((bsz, 2, D_MODEL, w_c), BF16))
        out_specs.append(pl.BlockSpec((None, 2, D_MODEL, w_c // n_split), lambda b, w: (b, 0, 0, w)))
    outs = pl.pallas_call(
        functools.partial(_attn_sample_body, n_new, n_split, compacted, emit),
        out_shape=out_shape,
        grid=(bsz, n_split),
        in_specs=[pl.BlockSpec((None, n_new, proj.shape[1]), lambda b, w: (b, 0, 0)),
                  pl.BlockSpec((None, n_new, kv.shape[1]), lambda b, w: (b, 0, 0))] + specs,
        out_specs=out_specs,
        scratch_shapes=[pltpu.VMEM((rows, 1), F32), pltpu.VMEM((rows, 1), F32), pltpu.VMEM((rows, D_MODEL), F32)],
        compiler_params=pltpu.CompilerParams(dimension_semantics=("arbitrary", "arbitrary"),
                                             vmem_limit_bytes=VMEM_LIMIT),
        name="attn_sample",
    )(proj3, kv3, *caches_t)
    o = outs[0].reshape(bsz * n_new, D_MODEL)
    return (o, outs[1]) if emit else o


def _b_post_body(final, *refs):
    if final:
        o_ref, z_ref, x_ref, wo_ref, fl_ref, out_ref = refs
    else:
        o_ref, z_ref, x_ref, wo_ref, out_ref = refs
    z = z_ref[...]
    gated = o_ref[...] * (z * _sigmoid(z))
    h = x_ref[...] + _dot(gated.astype(BF16), wo_ref[...])
    out_ref[...] = _rms(h, fl_ref[...]) if final else h


def _b_post(o, proj, x, w_out, final_ln):
    n = x.shape[0]
    tm = min(512, n)
    final = final_ln is not None
    row = pl.BlockSpec((tm, D_MODEL), lambda i: (i, 0))
    specs = [row, pl.BlockSpec((tm, D_MODEL), lambda i: (i, N_GROUPS)), row,
             pl.BlockSpec((D_MODEL, D_MODEL), lambda i: (0, 0))]
    args = [o, proj, x, w_out]
    if final:
        specs.append(pl.BlockSpec((1, D_MODEL), lambda i: (0, 0)))
        args.append(final_ln)
    return pl.pallas_call(
        functools.partial(_b_post_body, final),
        out_shape=jax.ShapeDtypeStruct((n, D_MODEL), F32),
        grid=(n // tm,),
        in_specs=specs,
        out_specs=row,
        compiler_params=pltpu.CompilerParams(dimension_semantics=("arbitrary",), vmem_limit_bytes=VMEM_LIMIT),
        name="b_post",
    )(*args)


def _kv_tail_t_body(x_ref, o_ref):
    o_ref[...] = x_ref[...].T


def _kv_tail_t(kv, bsz, seq_len, g, wc):
    tk = min(512, wc)
    first = (seq_len - wc) // tk
    kv3 = kv.reshape(bsz, seq_len, kv.shape[1])
    return pl.pallas_call(
        _kv_tail_t_body,
        out_shape=jax.ShapeDtypeStruct((bsz, 2, D_MODEL, wc), F32),
        grid=(bsz, 2, wc // tk),
        in_specs=[pl.BlockSpec((None, tk, D_MODEL), lambda b, kv_i, i: (b, first + i, kv_i * N_GROUPS + g))],
        out_specs=pl.BlockSpec((None, None, D_MODEL, tk), lambda b, kv_i, i: (b, kv_i, 0, i)),
        compiler_params=pltpu.CompilerParams(dimension_semantics=("arbitrary",) * 3, vmem_limit_bytes=VMEM_LIMIT),
        name="kv_tail_t",
    )(kv3)


def _rope_tables(pos):
    half = HEAD_DIM // 2
    inv = ROPE_THETA ** (-jnp.arange(half, dtype=F32) / half)
    ang = pos.astype(F32)[:, None] * inv[None, :]
    cos, sin = jnp.cos(ang), jnp.sin(ang)
    return jnp.concatenate([cos, cos, cos, cos], axis=1), jnp.concatenate([-sin, sin, -sin, sin], axis=1)


def _rwkv_stack(x, shift_init, wkv_init, n_seq, seq_len, layers):
    shifts, states = [], []
    v_first = None
    if seq_len % WKV_CHUNK == 0:
        chunk, rows_per_step, pad = WKV_CHUNK, 2, 0
    else:
        chunk, rows_per_step = WKV_SMALL_CHUNK, 2
        pad = (-seq_len) % chunk
    for l, w in enumerate(layers):
        (r, lw, k, v, kk, b, z), sh = _a_pre(x, shift_init[l], v_first, w, n_seq, seq_len)
        if l == 0:
            v_first = v
        seqs = [t.reshape(n_seq, seq_len, D_MODEL) for t in (r, lw, k, v, kk, b)]
        if pad:
            seqs = [jnp.pad(t, ((0, 0), (0, pad), (0, 0))) for t in seqs]
        y, st = _wkv(seqs, None if wkv_init is None else wkv_init[l], chunk, rows_per_step)
        y = y[:, :seq_len].reshape(n_seq * seq_len, D_MODEL)
        x = _a_post(y, r, k, v, z, x, w)
        shifts.append(sh)
        states.append(st)
    return x, jnp.stack(shifts), jnp.stack(states)


def kernel(x_prompt, x_sample, state_wkv, state_shift, cache_kv_g0, cache_kv_g1, cache_kv_g2, a_ln, a_mu, a_w_in, a_w0, a_w_lora_a, a_w_lora_b, a_a0, a_a_lora_a, a_a_lora_b, a_v0, a_v_lora_a, a_v_lora_b, a_k_k, a_k_a, a_r_k, a_gn_w, a_gn_b, a_w_out, kv_ln, w_kv, b_ln, b_w_in, b_w_out, final_ln):
    bp, s, _ = x_prompt.shape
    bs, t, _ = x_sample.shape
    n_a = a_ln.shape[0]
    n_b = b_ln.shape[0]

    def row(vec):
        return vec.reshape(1, D_MODEL)

    layers = []
    for l in range(n_a):
        w = dict(ln=row(a_ln[l]), mu=a_mu[l], w_in=a_w_in[l].astype(BF16), w0=row(a_w0[l]),
                 w_la=a_w_lora_a[l].astype(BF16), w_lb=a_w_lora_b[l].astype(BF16), a0=row(a_a0[l]),
                 a_la=a_a_lora_a[l].astype(BF16), a_lb=a_a_lora_b[l].astype(BF16),
                 k_k=row(a_k_k[l]), k_a=row(a_k_a[l]), r_k=row(a_r_k[l]), gn_w=row(a_gn_w[l]),
                 gn_b=row(a_gn_b[l]), w_out=a_w_out[l].astype(BF16))
        if l > 0:
            w.update(v0=row(a_v0[l - 1]), v_la=a_v_lora_a[l - 1].astype(BF16), v_lb=a_v_lora_b[l - 1].astype(BF16))
        layers.append(w)

    xp = x_prompt.reshape(bp * s, D_MODEL)
    xs = x_sample.reshape(bs * t, D_MODEL)
    zero_shift = jnp.zeros((n_a, bp, D_MODEL), F32)
    hp, shift_p, wkv_p = _rwkv_stack(xp, zero_shift, None, bp, s, layers)
    hs, shift_s, wkv_s = _rwkv_stack(xs, state_shift, state_wkv, bs, t, layers)

    cos_p, sin_p = _rope_tables(jnp.arange(s, dtype=jnp.int32))
    cos_s, sin_s = _rope_tables(jnp.tile(PAST_LEN + jnp.arange(t, dtype=jnp.int32), bs))
    w_kv_b = w_kv.astype(BF16)
    kv_ln_r = row(kv_ln)
    kvp = _proj(hp, kv_ln_r, w_kv_b, cos_p, sin_p, N_GROUPS)
    kvs = _proj(hs, kv_ln_r, w_kv_b, cos_s, sin_s, N_GROUPS)

    caches = tuple(jnp.transpose(c, (0, 2, 3, 4, 1)).reshape(bs, 2, D_MODEL, c.shape[1])
                   for c in (cache_kv_g0, cache_kv_g1, cache_kv_g2))
    for l in range(n_b):
        w_in = b_w_in[l].astype(BF16)
        w_out = b_w_out[l].astype(BF16)
        fl = row(final_ln) if l == n_b - 1 else None
        pj = _proj(hp, row(b_ln[l]), w_in, cos_p, sin_p, N_GROUPS)
        hp = _b_post(_attn_prompt(pj, kvp, bp, s), pj, hp, w_out, fl)
        pj = _proj(hs, row(b_ln[l]), w_in, cos_s, sin_s, N_GROUPS)
        if l == 0 and n_b > 1:
            o_s, compact = _attn_sample(pj, kvs, caches, bs, t, True)
            caches = caches[:-1] + (compact,)
        else:
            o_s = _attn_sample(pj, kvs, caches, bs, t, False)
        hs = _b_post(o_s, pj, hs, w_out, fl)

    y_prompt = hp.reshape(bp, s, D_MODEL)
    y_sample = hs.reshape(bs, t, D_MODEL)

    kvs3 = kvs.reshape(bs, t, 2, N_GROUPS, N_HEADS, HEAD_DIM)
    kv_out = []
    for g in range(N_GROUPS):
        wc = min(WINDOWS[g], s)
        rows_minor = _kv_tail_t(kvp, bp, s, g, wc).reshape(bp, 2, N_HEADS, HEAD_DIM, wc)
        kv_out.append(jnp.transpose(rows_minor, (0, 4, 1, 2, 3)))
        kv_out.append(kvs3[:, :, :, g])
    return (y_prompt, y_sample, wkv_p, wkv_s, shift_p, shift_s, *kv_out)
```

```python
import functools

import jax
import jax.numpy as jnp
from jax import lax
from jax.experimental import pallas as pl
from jax.experimental.pallas import tpu as pltpu

F32 = jnp.float32
BF16 = jnp.bfloat16

D_MODEL = 1024
HEAD_DIM = 64
N_HEADS = D_MODEL // HEAD_DIM
LANES = 128
N_PAIRS = D_MODEL // LANES
N_GROUPS = 3
WINDOWS = (128, 512, 2048)
DILATIONS = (1, 4, 16)
KEYS_PER_QUERY = 128
ATT_SCALE = HEAD_DIM ** -0.5
ROPE_THETA = 10000.0
NEG_INF = -1e30
GN_EPS = 64e-5
RMS_EPS = 1e-6
PAST_LEN = 2048
DECAY_LOG_SCALE = -0.6065306597126334
WKV_CHUNK = 64
WKV_SMALL_CHUNK = 8
ATTN_UNROLL = 4
ATTN_S_SPLIT = 2
COMPACT_DIL = 4
VMEM_LIMIT = 56 * 1024 * 1024

_HI = lax.Precision.HIGHEST


def _dot(a, b):
    return jnp.dot(a, b, preferred_element_type=F32)


def _dot_nt(a, b):
    return lax.dot_general(a, b, (((1,), (1,)), ((), ())), preferred_element_type=F32)


def _dot_tn(a, b):
    return lax.dot_general(a, b, (((0,), (0,)), ((), ())), preferred_element_type=F32)


def _rms(x, g):
    return x * lax.rsqrt(jnp.mean(x * x, axis=-1, keepdims=True) + RMS_EPS) * g


def _sigmoid(x):
    return 1.0 / (1.0 + jnp.exp(-x))


def _block_ones():
    r = lax.broadcasted_iota(jnp.int32, (LANES, LANES), 0) // HEAD_DIM
    c = lax.broadcasted_iota(jnp.int32, (LANES, LANES), 1) // HEAD_DIM
    return (r == c).astype(BF16)


def _head_sum(x, bo):
    outs = []
    for g in range(x.shape[1] // LANES):
        xg = x[:, g * LANES:(g + 1) * LANES]
        hi = xg.astype(BF16)
        lo = (xg - hi.astype(F32)).astype(BF16)
        outs.append(_dot(hi, bo) + _dot(lo, bo))
    return outs[0] if len(outs) == 1 else jnp.concatenate(outs, axis=1)


def _a_pre_body(has_vres, per_row_shift, seq_len, *refs):
    it = iter(refs)
    x_ref, sp_ref = next(it), next(it)
    vf_ref = next(it) if has_vres else None
    ln_ref, mu_ref, win_ref = next(it), next(it), next(it)
    w0_ref, wla_ref, wlb_ref = next(it), next(it), next(it)
    a0_ref, ala_ref, alb_ref = next(it), next(it), next(it)
    if has_vres:
        v0_ref, vla_ref, vlb_ref = next(it), next(it), next(it)
    kk_ref, ka_ref = next(it), next(it)
    r_o, lw_o, k_o, v_o, kk_o, b_o, z_o, sh_o = [next(it) for _ in range(8)]
    scr = next(it)

    tm = x_ref.shape[0]
    xn = _rms(x_ref[...], ln_ref[...])
    rolled = pltpu.roll(xn, 1, axis=0)
    row = lax.broadcasted_iota(jnp.int32, (tm, 1), 0)
    if per_row_shift:
        prev = jnp.where(row % seq_len == 0, sp_ref[...], rolled)
        for g in range(N_PAIRS):
            lanes = slice(g * LANES, (g + 1) * LANES)
            scr[g] = xn[:, lanes]
            sh_o[:, lanes] = scr[g, pl.ds(seq_len - 1, tm // seq_len, stride=seq_len), :]
    else:
        first = jnp.where(pl.program_id(1) == 0, sp_ref[...], scr[0:1, :])
        prev = jnp.where(row == 0, first, rolled)
        scr[0:1, :] = xn[tm - 1:tm, :]
        sh_o[...] = xn[tm - 1:tm, :]

    dx = prev - xn
    mu = mu_ref[...]

    def mixed(p):
        return xn + mu[p:p + 1, :] * dx

    xv = mixed(2)
    r = _dot(mixed(0).astype(BF16), win_ref[0])
    k = _dot(mixed(1).astype(BF16), win_ref[1])
    v = _dot(xv.astype(BF16), win_ref[2])
    z_o[...] = _dot(mixed(3).astype(BF16), win_ref[3])

    def lora(xm, la_ref, lb_ref, act):
        hdn = _dot(xm.astype(BF16), la_ref[...])
        if act:
            hdn = jnp.tanh(hdn)
        return _dot(hdn.astype(BF16), lb_ref[...])

    wl = w0_ref[...] + lora(mixed(4), wla_ref, wlb_ref, True)
    lw_o[...] = DECAY_LOG_SCALE * _sigmoid(wl)
    a = _sigmoid(a0_ref[...] + lora(mixed(5), ala_ref, alb_ref, False))
    if has_vres:
        v = v + (vf_ref[...] - v) * _sigmoid(v0_ref[...] + lora(xv, vla_ref, vlb_ref, False))

    bo = _block_ones()
    kkp = k * kk_ref[...]
    kk = kkp * lax.rsqrt(jnp.maximum(_head_sum(kkp * kkp, bo), 1e-24))
    r_o[...] = r
    k_o[...] = k * (1.0 + (a - 1.0) * ka_ref[...])
    v_o[...] = v
    kk_o[...] = kk
    b_o[...] = kk * a


def _a_pre(x, shift_prev, v_first, w, l, n_seq, seq_len):
    n = x.shape[0]
    has_vres = v_first is not None
    per_row_shift = seq_len < 256
    if per_row_shift:
        tm = n
        grid = (1, 1)
        sp = jnp.repeat(shift_prev, seq_len, axis=0)
        sp_spec = pl.BlockSpec((tm, D_MODEL), lambda b, i: (0, 0))
        sh_shape = jax.ShapeDtypeStruct((n_seq, D_MODEL), F32)
        sh_spec = pl.BlockSpec((n_seq, D_MODEL), lambda b, i: (0, 0))
        scr = pltpu.VMEM((N_PAIRS, tm, LANES), F32)
        tiles = 1
    else:
        tm = 256
        tiles = seq_len // tm
        grid = (n_seq, tiles)
        sp = shift_prev.reshape(n_seq, 1, D_MODEL)
        sp_spec = pl.BlockSpec((None, 1, D_MODEL), lambda b, i: (b, 0, 0))
        sh_shape = jax.ShapeDtypeStruct((n_seq, 1, D_MODEL), F32)
        sh_spec = pl.BlockSpec((None, 1, D_MODEL), lambda b, i: (b, 0, 0))
        scr = pltpu.VMEM((8, D_MODEL), F32)

    row_spec = pl.BlockSpec((tm, D_MODEL), lambda b, i: (b * tiles + i, 0))

    def of_layer(name, idx):
        arr = w[name]
        return arr, pl.BlockSpec((None,) + arr.shape[1:], lambda b, i: (idx,) + (0,) * (arr.ndim - 1))

    picks = [of_layer(name, l) for name in ('ln', 'mu', 'w_in', 'w0', 'w_la', 'w_lb', 'a0', 'a_la', 'a_lb')]
    if has_vres:
        picks += [of_layer(name, l - 1) for name in ('v0', 'v_la', 'v_lb')]
    picks += [of_layer(name, l) for name in ('k_k', 'k_a')]
    args = [x, sp] + ([v_first] if has_vres else []) + [a for a, _ in picks]
    specs = [row_spec, sp_spec] + ([row_spec] if has_vres else []) + [s for _, s in picks]

    full = jax.ShapeDtypeStruct((n, D_MODEL), F32)
    outs = pl.pallas_call(
        functools.partial(_a_pre_body, has_vres, per_row_shift, seq_len),
        out_shape=[full] * 7 + [sh_shape],
        grid=grid,
        in_specs=specs,
        out_specs=[row_spec] * 7 + [sh_spec],
        scratch_shapes=[scr],
        compiler_params=pltpu.CompilerParams(dimension_semantics=("arbitrary", "arbitrary"),
                                             vmem_limit_bytes=VMEM_LIMIT),
        name="a_pre",
    )(*args)
    return outs[:7], outs[7].reshape(n_seq, D_MODEL)


def _wkv_body(zero_init, L, *refs):
    it = iter(refs)
    r_ref, lw_ref, k_ref, v_ref, kk_ref, b_ref = [next(it) for _ in range(6)]
    st_ref = None if zero_init else next(it)
    y_ref, so_ref = next(it), next(it)
    sbd = next(it)

    n_b = r_ref.shape[0]
    c = pl.program_id(1)
    nc = pl.num_programs(1)
    half = HEAD_DIM
    chains = [(bi, p) for bi in range(n_b) for p in range(N_PAIRS)]

    @pl.when(c == 0)
    def _():
        if zero_init:
            sbd[...] = jnp.zeros(sbd.shape, F32)
        else:
            zeros = jnp.zeros((half, half), F32)
            for bi, p in chains:
                top = jnp.concatenate([st_ref[bi, 2 * p], zeros], axis=1)
                bot = jnp.concatenate([zeros, st_ref[bi, 2 * p + 1]], axis=1)
                sbd[bi, p] = jnp.concatenate([top, bot], axis=0)

    rr = lax.broadcasted_iota(jnp.int32, (L, L), 0)
    cc = lax.broadcasted_iota(jnp.int32, (L, L), 1)
    tri = (rr >= cc).astype(F32)
    lane = lax.broadcasted_iota(jnp.int32, (1, LANES), 1)
    h0 = lane < half
    r2 = lax.broadcasted_iota(jnp.int32, (2 * L, 2 * L), 0)
    c2 = lax.broadcasted_iota(jnp.int32, (2 * L, 2 * L), 1)
    strict = r2 > c2
    incl = r2 >= c2
    eye = (r2 == c2).astype(F32)
    blk = min(16, L)
    diag_blk = (r2 // blk) == (c2 // blk)

    def split(x):
        return jnp.concatenate([jnp.where(h0, x, 0.0), jnp.where(h0, 0.0, x)], axis=0).astype(BF16)

    def mm(a, b):
        return _dot(a.astype(BF16), b.astype(BF16))

    def neumann(a, n_terms):
        x = {ch: eye - a[ch] for ch in chains}
        pw, k = a, 2
        while k < n_terms:
            pw = {ch: mm(pw[ch], pw[ch]) for ch in chains}
            x = {ch: mm(x[ch], eye + pw[ch]) for ch in chains}
            k *= 2
        return x

    lhs4, bts, kls, v2s, bks, e_gls = {}, {}, {}, {}, {}, {}
    for bi in range(n_b):
        lw = lw_ref[bi]
        g_incl = jnp.dot(tri, lw, precision=_HI, preferred_element_type=F32)
        e_g = jnp.exp(g_incl)
        e_ng = jnp.exp(-g_incl)
        e_gl = e_g[L - 1:L, :]
        kt = kk_ref[bi] * jnp.exp(g_incl - lw)
        rt = r_ref[bi] * e_g
        bt = b_ref[bi] * e_ng
        kl = k_ref[bi] * e_ng
        bh = bt * e_gl
        kh = kl * e_gl
        vv = v_ref[bi]
        for p in range(N_PAIRS):
            sl = slice(p * LANES, (p + 1) * LANES)
            lhs4[bi, p] = jnp.concatenate([split(kt[:, sl]), split(rt[:, sl])], axis=0)
            bts[bi, p] = split(bt[:, sl])
            kls[bi, p] = split(kl[:, sl])
            v2s[bi, p] = split(vv[:, sl])
            bks[bi, p] = jnp.concatenate([split(bh[:, sl]), split(kh[:, sl])], axis=0)
            e_gls[bi, p] = e_gl[:, sl]

    wide_ok = (2 * L) % LANES == 0
    if wide_ok:
        aa = {ch: _dot_nt(lhs4[ch], jnp.concatenate([bts[ch], kls[ch]], axis=0)) for ch in chains}
        ab = {ch: aa[ch][:, 0:2 * L] for ch in chains}
        ak = {ch: aa[ch][:, 2 * L:4 * L] for ch in chains}
    else:
        ab = {ch: _dot_nt(lhs4[ch], bts[ch]) for ch in chains}
        ak = {ch: _dot_nt(lhs4[ch], kls[ch]) for ch in chains}
    a_kb = {ch: jnp.where(strict, ab[ch][0:2 * L], 0.0) for ch in chains}
    a_rb = {ch: jnp.where(incl, ab[ch][2 * L:4 * L], 0.0).astype(BF16) for ch in chains}
    a_kk = {ch: jnp.where(strict, ak[ch][0:2 * L], 0.0).astype(BF16) for ch in chains}
    a_rk = {ch: jnp.where(incl, ak[ch][2 * L:4 * L], 0.0).astype(BF16) for ch in chains}
    if L > blk:
        a_d = {ch: jnp.where(diag_blk, a_kb[ch], 0.0) for ch in chains}
        x_d = neumann(a_d, blk)
        nn = {ch: mm(x_d[ch], a_kb[ch] - a_d[ch]) for ch in chains}
        x_n = neumann(nn, L // blk)
        x_f = {ch: mm(x_n[ch], x_d[ch]).astype(BF16) for ch in chains}
    else:
        x_f = {ch: v.astype(BF16) for ch, v in neumann(a_kb, L).items()}

    s_old = {ch: sbd[ch[0], ch[1]] for ch in chains}
    pp = {ch: _dot_nt(lhs4[ch], s_old[ch].astype(BF16)) for ch in chains}
    rhs = {ch: -(pp[ch][0:2 * L] + _dot(a_kk[ch], v2s[ch])) for ch in chains}
    u2 = {ch: _dot(x_f[ch], rhs[ch].astype(BF16)).astype(BF16) for ch in chains}
    for ch in chains:
        bi, p = ch
        uv = jnp.concatenate([u2[ch], v2s[ch]], axis=0)
        if wide_ok:
            ys = pp[ch][2 * L:4 * L] + _dot(jnp.concatenate([a_rb[ch], a_rk[ch]], axis=1), uv)
        else:
            ys = pp[ch][2 * L:4 * L] + _dot(a_rb[ch], u2[ch]) + _dot(a_rk[ch], v2s[ch])
        y_ref[bi, :, p * LANES:(p + 1) * LANES] = ys[0:L] + ys[L:2 * L]
        sbd[bi, p] = s_old[ch] * e_gls[ch] + _dot_tn(uv, bks[ch])

    @pl.when(c == nc - 1)
    def _():
        for bi, p in chains:
            s_p = sbd[bi, p]
            so_ref[bi, 2 * p] = s_p[0:half, 0:half]
            so_ref[bi, 2 * p + 1] = pltpu.roll(s_p[half:2 * half, :], half, axis=1)[:, 0:half]


def _wkv(seqs, state0, chunk, rows_per_step):
    bsz, t, _ = seqs[0].shape
    zero_init = state0 is None
    nb = rows_per_step
    tok = pl.BlockSpec((nb, chunk, D_MODEL), lambda b, c: (b, c, 0))
    st = pl.BlockSpec((nb, N_HEADS, HEAD_DIM, HEAD_DIM), lambda b, c: (b, 0, 0, 0))
    args = list(seqs) + ([] if zero_init else [state0])
    specs = [tok] * 6 + ([] if zero_init else [st])
    return pl.pallas_call(
        functools.partial(_wkv_body, zero_init, chunk),
        out_shape=[jax.ShapeDtypeStruct((bsz, t, D_MODEL), F32),
                   jax.ShapeDtypeStruct((bsz, N_HEADS, HEAD_DIM, HEAD_DIM), F32)],
        grid=(bsz // nb, t // chunk),
        in_specs=specs,
        out_specs=[tok, st],
        scratch_shapes=[pltpu.VMEM((nb, N_PAIRS, LANES, LANES), F32)],
        compiler_params=pltpu.CompilerParams(dimension_semantics=("arbitrary", "arbitrary"),
                                             vmem_limit_bytes=VMEM_LIMIT),
        name="wkv",
    )(*args)


def _a_post_body(y_ref, r_ref, k_ref, v_ref, z_ref, x_ref, rk_ref, gw_ref, gb_ref, wo_ref, o_ref):
    bo = _block_ones()
    y = y_ref[...]
    inv_c = 1.0 / HEAD_DIM
    yc = y - _head_sum(y, bo) * inv_c
    var = _head_sum(yc * yc, bo) * inv_c
    yn = yc * lax.rsqrt(var + GN_EPS) * gw_ref[...] + gb_ref[...]
    yn = yn + _head_sum(r_ref[...] * k_ref[...] * rk_ref[...], bo) * v_ref[...]
    z = z_ref[...]
    gated = yn * (z * _sigmoid(z))
    o_ref[...] = x_ref[...] + _dot(gated.astype(BF16), wo_ref[...])


def _a_post(y, r, k, v, z, x, w, l):
    n = x.shape[0]
    tm = 512
    row = pl.BlockSpec((tm, D_MODEL), lambda i: (i, 0))
    names = ('r_k', 'gn_w', 'gn_b', 'w_out')
    picks = [pl.BlockSpec((None,) + w[nm].shape[1:], lambda i, nd=w[nm].ndim: (l,) + (0,) * (nd - 1)) for nm in names]
    return pl.pallas_call(
        _a_post_body,
        out_shape=jax.ShapeDtypeStruct((n, D_MODEL), F32),
        grid=(n // tm,),
        in_specs=[row] * 6 + picks,
        out_specs=row,
        compiler_params=pltpu.CompilerParams(dimension_semantics=("arbitrary",), vmem_limit_bytes=VMEM_LIMIT),
        name="a_post",
    )(y, r, k, v, z, x, *[w[nm] for nm in names])


def _rope_tile(t, cos_t, sin_t):
    lane = lax.broadcasted_iota(jnp.int32, (1, LANES), 1)
    first_half = (lane % HEAD_DIM) < (HEAD_DIM // 2)
    outs = []
    for g in range(t.shape[1] // LANES):
        xg = t[:, g * LANES:(g + 1) * LANES]
        partner = jnp.where(first_half, pltpu.roll(xg, LANES - HEAD_DIM // 2, axis=1),
                            pltpu.roll(xg, HEAD_DIM // 2, axis=1))
        outs.append(xg * cos_t + partner * sin_t)
    return jnp.concatenate(outs, axis=1)


def _proj_body(n_rope, pair_major, x_ref, ln_ref, w_ref, cos_ref, sin_ref, o_ref):
    xn = _rms(x_ref[...], ln_ref[...]).astype(BF16)
    for j in range(w_ref.shape[1] // D_MODEL):
        cols = slice(j * D_MODEL, (j + 1) * D_MODEL)
        t = _dot(xn, w_ref[:, cols])
        if j < n_rope:
            t = _rope_tile(t, cos_ref[...], sin_ref[...])
        if pair_major:
            for p in range(N_PAIRS):
                o_ref[j * N_PAIRS + p] = t[:, p * LANES:(p + 1) * LANES]
        else:
            o_ref[:, cols] = t


def _proj(x, ln, w, cos_t, sin_t, n_rope, seqs=None):
    n = x.shape[0]
    n_out = w.shape[1]
    tm = min(256, n)
    n_tab = cos_t.shape[0] // tm
    tab = pl.BlockSpec((tm, LANES), lambda i: (i % n_tab, 0))
    if seqs is None:
        out_shape = jax.ShapeDtypeStruct((n, n_out), F32)
        out_spec = pl.BlockSpec((tm, n_out), lambda i: (i, 0))
    else:
        bsz, seq_len = seqs
        tiles = seq_len // tm
        out_shape = jax.ShapeDtypeStruct((bsz, n_out // LANES, seq_len, LANES), F32)
        out_spec = pl.BlockSpec((None, n_out // LANES, tm, LANES), lambda i: (i // tiles, 0, i % tiles, 0))
    return pl.pallas_call(
        functools.partial(_proj_body, n_rope, seqs is not None),
        out_shape=out_shape,
        grid=(n // tm,),
        in_specs=[pl.BlockSpec((tm, D_MODEL), lambda i: (i, 0)),
                  pl.BlockSpec((1, D_MODEL), lambda i: (0, 0)),
                  pl.BlockSpec((D_MODEL, n_out), lambda i: (0, 0)),
                  tab, tab],
        out_specs=out_spec,
        compiler_params=pltpu.CompilerParams(dimension_semantics=("arbitrary",), vmem_limit_bytes=VMEM_LIMIT),
        name="proj",
    )(x, ln, w, cos_t, sin_t)


def _attn_prompt_body(seq_len, q0, q1, q2, k0, k1, k2, v0, v1, v2, o_ref, m_s, l_s, a_s):
    nq = KEYS_PER_QUERY
    lane = lax.broadcasted_iota(jnp.int32, (1, LANES), 1)
    h0 = lane < HEAD_DIM
    qi = lax.broadcasted_iota(jnp.int32, (nq, nq), 0)
    ki = lax.broadcasted_iota(jnp.int32, (nq, nq), 1)
    cur_mask = ki <= qi
    prev_mask = ki > qi

    heads = (h0, jnp.logical_not(h0))

    def run_blocks(g, q_ref, k_ref, v_ref, d, blocks):
        rows, qs, kbs, vbs, masks = [], [], [], [], []
        cur = []
        for c, nb, prev in blocks:
            rw = pl.ds(c + d * nq * nb, nq, stride=d)
            cur.append((k_ref[rw, :].astype(BF16), v_ref[rw, :].astype(BF16)))
            rows.append(rw)
        for i, (c, nb, prev) in enumerate(blocks):
            kb, vb = cur[i]
            mask = cur_mask
            if prev is not None:
                if prev is True and i > 0:
                    kp, vp = cur[i - 1]
                else:
                    pb = nb - 1 if prev is True else jnp.maximum(nb - 1, 0)
                    prow = pl.ds(c + d * nq * pb, nq, stride=d)
                    kp, vp = k_ref[prow, :].astype(BF16), v_ref[prow, :].astype(BF16)
                kb = jnp.concatenate([kp, kb], axis=0)
                vb = jnp.concatenate([vp, vb], axis=0)
                pm = prev_mask if prev is True else jnp.logical_and(prev_mask, prev)
                mask = jnp.concatenate([pm, cur_mask], axis=1)
            rw = rows[i]
            qs.append(q_ref[rw, :] * ATT_SCALE)
            kbs.append(kb)
            vbs.append(vb)
            masks.append(mask)
        n = len(blocks)
        pairs = [(i, h) for i in range(n) for h in range(2)]
        s = {(i, h): _dot_nt(jnp.where(heads[h], qs[i], 0.0).astype(BF16), kbs[i]) for i, h in pairs}
        s = {ih: jnp.where(masks[ih[0]], s[ih], NEG_INF) for ih in pairs}
        mx = {ih: jnp.max(s[ih], axis=-1, keepdims=True) for ih in pairs}
        pr = {ih: jnp.exp(s[ih] - mx[ih]) for ih in pairs}
        ls = {ih: jnp.sum(pr[ih], axis=-1, keepdims=True) for ih in pairs}
        acc = {(i, h): _dot(pr[i, h].astype(BF16), vbs[i]) for i, h in pairs}
        for i in range(n):
            m_b = jnp.where(h0, mx[i, 0], mx[i, 1])
            l_b = jnp.where(h0, ls[i, 0], ls[i, 1])
            a_b = jnp.where(h0, acc[i, 0], acc[i, 1])
            if g > 0:
                m_o, l_o, a_o = m_s[rows[i], :], l_s[rows[i], :], a_s[rows[i], :]
                m_n = jnp.maximum(m_o, m_b)
                w_o = jnp.exp(m_o - m_n)
                w_b = jnp.exp(m_b - m_n)
                m_b = m_n
                l_b = w_o * l_o + w_b * l_b
                a_b = w_o * a_o + w_b * a_b
            if g == N_GROUPS - 1:
                o_ref[rows[i], :] = a_b / l_b
            else:
                m_s[rows[i], :] = m_b
                l_s[rows[i], :] = l_b
                a_s[rows[i], :] = a_b

    u = ATTN_UNROLL
    for g, (q_ref, k_ref, v_ref) in enumerate(((q0, k0, v0), (q1, k1, v1), (q2, k2, v2))):
        d = DILATIONS[g]
        n_blk = seq_len // (d * nq)
        if n_blk == 1:
            def classes(i, carry, g=g, q_ref=q_ref, k_ref=k_ref, v_ref=v_ref, d=d):
                run_blocks(g, q_ref, k_ref, v_ref, d, [(i * u + j, 0, None) for j in range(u)])
                return carry
            lax.fori_loop(0, d // u, classes, 0)
        elif n_blk == u:
            def one_class(c, carry, g=g, q_ref=q_ref, k_ref=k_ref, v_ref=v_ref, d=d):
                run_blocks(g, q_ref, k_ref, v_ref, d, [(c, nb, True if nb else None) for nb in range(u)])
                return carry
            lax.fori_loop(0, d, one_class, 0)
        else:
            def blocks(i, carry, g=g, q_ref=q_ref, k_ref=k_ref, v_ref=v_ref, d=d):
                run_blocks(g, q_ref, k_ref, v_ref, d,
                           [(0, i * u + j, (i > 0) if j == 0 else True) for j in range(u)])
                return carry
            assert d == 1 and n_blk % u == 0
            lax.fori_loop(0, n_blk // u, blocks, 0)


def _attn_prompt(proj, kv):
    bsz, _, seq_len, _ = proj.shape

    def col(base):
        return pl.BlockSpec((None, None, seq_len, LANES), lambda b, p: (b, base + p, 0, 0))

    q_specs = [col(g * N_PAIRS) for g in range(N_GROUPS)]
    k_specs = [col(g * N_PAIRS) for g in range(N_GROUPS)]
    v_specs = [col((N_GROUPS + g) * N_PAIRS) for g in range(N_GROUPS)]
    return pl.pallas_call(
        functools.partial(_attn_prompt_body, seq_len),
        out_shape=jax.ShapeDtypeStruct((bsz, N_PAIRS, seq_len, LANES), F32),
        grid=(bsz, N_PAIRS),
        in_specs=q_specs + k_specs + v_specs,
        out_specs=col(0),
        scratch_shapes=[pltpu.VMEM((seq_len, LANES), F32)] * 3,
        compiler_params=pltpu.CompilerParams(dimension_semantics=("arbitrary", "arbitrary"),
                                             vmem_limit_bytes=VMEM_LIMIT),
        name="attn_prompt",
    )(proj, proj, proj, kv, kv, kv, kv, kv, kv)


def _attn_sample_body(n_new, n_split, compacted, emit, *refs):
    if emit:
        q_ref, kvn_ref, c0_ref, c1_ref, c2_ref, o_ref, cc_ref, m_s, l_s, acc_s = refs
    else:
        q_ref, kvn_ref, c0_ref, c1_ref, c2_ref, o_ref, m_s, l_s, acc_s = refs
    w = pl.program_id(1)
    rows = n_new * N_HEADS
    rid = lax.broadcasted_iota(jnp.int32, (rows, D_MODEL), 0)
    cid = lax.broadcasted_iota(jnp.int32, (rows, D_MODEL), 1)
    own_head = (rid % N_HEADS) == (cid // HEAD_DIM)
    jrow = lax.broadcasted_iota(jnp.int32, (rows, 1), 0) // N_HEADS

    def q_rows(g):
        qg = q_ref[:, g * D_MODEL:(g + 1) * D_MODEL] * ATT_SCALE
        full = jnp.concatenate([jnp.broadcast_to(qg[j:j + 1], (N_HEADS, D_MODEL)) for j in range(n_new)], axis=0)
        return jnp.where(own_head, full, 0.0)

    def absorb_cache(state, d, qb, kt, vt, first_row):
        s = _dot(qb.astype(BF16), kt)
        r = first_row + lax.broadcasted_iota(jnp.int32, (1, s.shape[1]), 1)
        s = jnp.where((r % d == jrow % d) & (r >= jrow + d), s, NEG_INF)
        m_c = jnp.max(s, axis=-1, keepdims=True)
        if state is None:
            m_n, alpha = m_c, None
        else:
            m_n = jnp.maximum(state[0], m_c)
            alpha = jnp.exp(state[0] - m_n)
        e = jnp.exp(s - m_n)
        l_c = jnp.sum(e, axis=-1, keepdims=True)
        a_c = _dot_nt(e.astype(BF16), vt)
        if state is None:
            return m_n, l_c, a_c
        return m_n, alpha * state[1] + l_c, alpha * state[2] + a_c

    def absorb_new(state, g, qb):
        d = DILATIONS[g]
        k_new = kvn_ref[:, g * D_MODEL:(g + 1) * D_MODEL]
        v_new = kvn_ref[:, (N_GROUPS + g) * D_MODEL:(N_GROUPS + g + 1) * D_MODEL]
        s = []
        for i in range(n_new):
            ok = (jrow >= i) & ((jrow - i) % d == 0)
            s.append(jnp.where(ok, jnp.sum(qb * k_new[i:i + 1], axis=-1, keepdims=True), NEG_INF))
        m_n = functools.reduce(jnp.maximum, s, state[0])
        alpha = jnp.exp(state[0] - m_n)
        l_n, a_n = alpha * state[1], alpha * state[2]
        for i in range(n_new):
            e = jnp.exp(s[i] - m_n)
            l_n = l_n + e
            a_n = a_n + e * v_new[i:i + 1]
        return m_n, l_n, a_n

    @pl.when(w == 0)
    def _():
        st = None
        for g, c_ref in ((0, c0_ref), (1, c1_ref)):
            qb = q_rows(g)
            st = absorb_cache(st, DILATIONS[g], qb, c_ref[0].astype(BF16), c_ref[1].astype(BF16), 0)
            st = absorb_new(st, g, qb)
        m_s[...], l_s[...], acc_s[...] = absorb_new(st, N_GROUPS - 1, q_rows(N_GROUPS - 1))

    wide = c2_ref.shape[2]
    kt, vt = c2_ref[0].astype(BF16), c2_ref[1].astype(BF16)
    lane_dil = COMPACT_DIL if compacted else DILATIONS[N_GROUPS - 1]
    m_s[...], l_s[...], acc_s[...] = absorb_cache((m_s[...], l_s[...], acc_s[...]), lane_dil,
                                                  q_rows(N_GROUPS - 1), kt, vt, w * wide)
    if emit:
        slab = LANES * COMPACT_DIL
        rr = lax.broadcasted_iota(jnp.int32, (slab, LANES), 0)
        cc = lax.broadcasted_iota(jnp.int32, (slab, LANES), 1)
        sel = (rr == DILATIONS[N_GROUPS - 1] * (cc // COMPACT_DIL) + cc % COMPACT_DIL).astype(BF16)
        for i, x in enumerate((kt, vt)):
            for k in range(wide // slab):
                cc_ref[i, :, k * LANES:(k + 1) * LANES] = _dot(x[:, k * slab:(k + 1) * slab], sel).astype(BF16)

    @pl.when(w == n_split - 1)
    def _():
        o = jnp.where(own_head, acc_s[...] / l_s[...], 0.0)
        o_ref[...] = jnp.sum(o.reshape(n_new, N_HEADS, D_MODEL), axis=1)


def _attn_sample(proj, kv, caches_t, bsz, n_new, emit):
    proj3 = proj.reshape(bsz, n_new, proj.shape[1])
    kv3 = kv.reshape(bsz, n_new, kv.shape[1])
    last = N_GROUPS - 1
    w_last = caches_t[last].shape[3]
    compacted = w_last != WINDOWS[last]
    assert n_new <= COMPACT_DIL and not (compacted and emit)
    assert w_last == (WINDOWS[last] // DILATIONS[last] * COMPACT_DIL if compacted else WINDOWS[last])
    n_split = 1 if compacted else ATTN_S_SPLIT
    specs = [pl.BlockSpec((None, 2, D_MODEL, WINDOWS[g]), lambda b, w: (b, 0, 0, 0)) for g in range(last)]
    specs.append(pl.BlockSpec((None, 2, D_MODEL, w_last // n_split), lambda b, w: (b, 0, 0, w)))
    for g in range(last):
        assert caches_t[g].shape[3] == WINDOWS[g]
    rows = n_new * N_HEADS
    out_shape = [jax.ShapeDtypeStruct((bsz, n_new, D_MODEL), F32)]
    out_specs = [pl.BlockSpec((None, n_new, D_MODEL), lambda b, w: (b, 0, 0))]
    if emit:
        w_c = w_last // DILATIONS[last] * COMPACT_DIL
        out_shape.append(jax.ShapeDtypeStruct((bsz, 2, D_MODEL, w_c), BF16))
        out_specs.append(pl.BlockSpec((None, 2, D_MODEL, w_c // n_split), lambda b, w: (b, 0, 0, w)))
    outs = pl.pallas_call(
        functools.partial(_attn_sample_body, n_new, n_split, compacted, emit),
        out_shape=out_shape,
        grid=(bsz, n_split),
        in_specs=[pl.BlockSpec((None, n_new, proj.shape[1]), lambda b, w: (b, 0, 0)),
                  pl.BlockSpec((None, n_new, kv.shape[1]), lambda b, w: (b, 0, 0))] + specs,
        out_specs=out_specs,
        scratch_shapes=[pltpu.VMEM((rows, 1), F32), pltpu.VMEM((rows, 1), F32), pltpu.VMEM((rows, D_MODEL), F32)],
        compiler_params=pltpu.CompilerParams(dimension_semantics=("arbitrary", "arbitrary"),
                                             vmem_limit_bytes=VMEM_LIMIT),
        name="attn_sample",
    )(proj3, kv3, *caches_t)
    o = outs[0].reshape(bsz * n_new, D_MODEL)
    return (o, outs[1]) if emit else o


def _wide(ref):
    if len(ref.shape) == 2:
        return ref[...]
    return jnp.concatenate([ref[p] for p in range(ref.shape[0])], axis=1)


def _b_post_body(final, *refs):
    if final:
        o_ref, z_ref, x_ref, wo_ref, fl_ref, out_ref = refs
    else:
        o_ref, z_ref, x_ref, wo_ref, out_ref = refs
    z = _wide(z_ref)
    gated = _wide(o_ref) * (z * _sigmoid(z))
    h = x_ref[...] + _dot(gated.astype(BF16), wo_ref[...])
    out_ref[...] = _rms(h, fl_ref[...]) if final else h


def _b_post(o, proj, x, w_out, final_ln):
    n = x.shape[0]
    tm = min(512, n)
    final = final_ln is not None
    row = pl.BlockSpec((tm, D_MODEL), lambda i: (i, 0))
    if o.ndim == 2:
        o_spec, z_spec = row, pl.BlockSpec((tm, D_MODEL), lambda i: (i, N_GROUPS))
    else:
        tiles = o.shape[2] // tm
        o_spec = pl.BlockSpec((None, N_PAIRS, tm, LANES), lambda i: (i // tiles, 0, i % tiles, 0))
        z_spec = pl.BlockSpec((None, N_PAIRS, tm, LANES), lambda i: (i // tiles, N_GROUPS, i % tiles, 0))
    specs = [o_spec, z_spec, row, pl.BlockSpec((D_MODEL, D_MODEL), lambda i: (0, 0))]
    args = [o, proj, x, w_out]
    if final:
        specs.append(pl.BlockSpec((1, D_MODEL), lambda i: (0, 0)))
        args.append(final_ln)
    return pl.pallas_call(
        functools.partial(_b_post_body, final),
        out_shape=jax.ShapeDtypeStruct((n, D_MODEL), F32),
        grid=(n // tm,),
        in_specs=specs,
        out_specs=row,
        compiler_params=pltpu.CompilerParams(dimension_semantics=("arbitrary",), vmem_limit_bytes=VMEM_LIMIT),
        name="b_post",
    )(*args)


def _kv_tail_t_body(x_ref, o_ref):
    o_ref[...] = _wide(x_ref).T


def _kv_tail_t(kv, g, wc):
    bsz, _, seq_len, _ = kv.shape
    tk = min(512, wc)
    first = (seq_len - wc) // tk
    return pl.pallas_call(
        _kv_tail_t_body,
        out_shape=jax.ShapeDtypeStruct((bsz, 2, D_MODEL, wc), F32),
        grid=(bsz, 2, wc // tk),
        in_specs=[pl.BlockSpec((None, N_PAIRS, tk, LANES),
                               lambda b, kv_i, i: (b, kv_i * N_GROUPS + g, first + i, 0))],
        out_specs=pl.BlockSpec((None, None, D_MODEL, tk), lambda b, kv_i, i: (b, kv_i, 0, i)),
        compiler_params=pltpu.CompilerParams(dimension_semantics=("arbitrary",) * 3, vmem_limit_bytes=VMEM_LIMIT),
        name="kv_tail_t",
    )(kv)


def _rope_tables(pos):
    half = HEAD_DIM // 2
    inv = ROPE_THETA ** (-jnp.arange(half, dtype=F32) / half)
    ang = pos.astype(F32)[:, None] * inv[None, :]
    cos, sin = jnp.cos(ang), jnp.sin(ang)
    return jnp.concatenate([cos, cos, cos, cos], axis=1), jnp.concatenate([-sin, sin, -sin, sin], axis=1)


def _rwkv_stack(x, shift_init, wkv_init, n_seq, seq_len, w, n_layers):
    shifts, states = [], []
    v_first = None
    if seq_len % WKV_CHUNK == 0:
        chunk, rows_per_step, pad = WKV_CHUNK, 2, 0
    else:
        chunk, rows_per_step = WKV_SMALL_CHUNK, 2
        pad = (-seq_len) % chunk
    for l in range(n_layers):
        (r, lw, k, v, kk, b, z), sh = _a_pre(x, shift_init[l], v_first, w, l, n_seq, seq_len)
        if l == 0:
            v_first = v
        seqs = [t.reshape(n_seq, seq_len, D_MODEL) for t in (r, lw, k, v, kk, b)]
        if pad:
            seqs = [jnp.pad(t, ((0, 0), (0, pad), (0, 0))) for t in seqs]
        y, st = _wkv(seqs, None if wkv_init is None else wkv_init[l], chunk, rows_per_step)
        y = y[:, :seq_len].reshape(n_seq * seq_len, D_MODEL)
        x = _a_post(y, r, k, v, z, x, w, l)
        shifts.append(sh)
        states.append(st)
    return x, jnp.stack(shifts), jnp.stack(states)


def kernel(x_prompt, x_sample, state_wkv, state_shift, cache_kv_g0, cache_kv_g1, cache_kv_g2, a_ln, a_mu, a_w_in, a_w0, a_w_lora_a, a_w_lora_b, a_a0, a_a_lora_a, a_a_lora_b, a_v0, a_v_lora_a, a_v_lora_b, a_k_k, a_k_a, a_r_k, a_gn_w, a_gn_b, a_w_out, kv_ln, w_kv, b_ln, b_w_in, b_w_out, final_ln):
    bp, s, _ = x_prompt.shape
    bs, t, _ = x_sample.shape
    n_a = a_ln.shape[0]
    n_b = b_ln.shape[0]

    def row(vec):
        return vec.reshape(1, D_MODEL)

    def rows(stacked):
        return stacked.reshape(stacked.shape[0], 1, D_MODEL)

    w_a = dict(ln=rows(a_ln), mu=a_mu, w_in=a_w_in.astype(BF16), w0=rows(a_w0),
               w_la=a_w_lora_a.astype(BF16), w_lb=a_w_lora_b.astype(BF16), a0=rows(a_a0),
               a_la=a_a_lora_a.astype(BF16), a_lb=a_a_lora_b.astype(BF16),
               v0=rows(a_v0), v_la=a_v_lora_a.astype(BF16), v_lb=a_v_lora_b.astype(BF16),
               k_k=rows(a_k_k), k_a=rows(a_k_a), r_k=rows(a_r_k), gn_w=rows(a_gn_w),
               gn_b=rows(a_gn_b), w_out=a_w_out.astype(BF16))

    xp = x_prompt.reshape(bp * s, D_MODEL)
    xs = x_sample.reshape(bs * t, D_MODEL)
    zero_shift = jnp.zeros((n_a, bp, D_MODEL), F32)
    hp, shift_p, wkv_p = _rwkv_stack(xp, zero_shift, None, bp, s, w_a, n_a)
    hs, shift_s, wkv_s = _rwkv_stack(xs, state_shift, state_wkv, bs, t, w_a, n_a)

    cos_p, sin_p = _rope_tables(jnp.arange(s, dtype=jnp.int32))
    cos_s, sin_s = _rope_tables(jnp.tile(PAST_LEN + jnp.arange(t, dtype=jnp.int32), bs))
    w_kv_b = w_kv.astype(BF16)
    kv_ln_r = row(kv_ln)
    kvp = _proj(hp, kv_ln_r, w_kv_b, cos_p, sin_p, N_GROUPS, (bp, s))
    kvs = _proj(hs, kv_ln_r, w_kv_b, cos_s, sin_s, N_GROUPS)

    caches = tuple(jnp.transpose(c, (0, 2, 3, 4, 1)).reshape(bs, 2, D_MODEL, c.shape[1])
                   for c in (cache_kv_g0, cache_kv_g1, cache_kv_g2))
    for l in range(n_b):
        w_in = b_w_in[l].astype(BF16)
        w_out = b_w_out[l].astype(BF16)
        fl = row(final_ln) if l == n_b - 1 else None
        pj = _proj(hp, row(b_ln[l]), w_in, cos_p, sin_p, N_GROUPS, (bp, s))
        hp = _b_post(_attn_prompt(pj, kvp), pj, hp, w_out, fl)
        pj = _proj(hs, row(b_ln[l]), w_in, cos_s, sin_s, N_GROUPS)
        if l == 0 and n_b > 1:
            o_s, compact = _attn_sample(pj, kvs, caches, bs, t, True)
            caches = caches[:-1] + (compact,)
        else:
            o_s = _attn_sample(pj, kvs, caches, bs, t, False)
        hs = _b_post(o_s, pj, hs, w_out, fl)

    y_prompt = hp.reshape(bp, s, D_MODEL)
    y_sample = hs.reshape(bs, t, D_MODEL)

    kvs3 = kvs.reshape(bs, t, 2, N_GROUPS, N_HEADS, HEAD_DIM)
    kv_out = []
    for g in range(N_GROUPS):
        wc = min(WINDOWS[g], s)
        rows_minor = _kv_tail_t(kvp, g, wc).reshape(bp, 2, N_HEADS, HEAD_DIM, wc)
        kv_out.append(jnp.transpose(rows_minor, (0, 4, 1, 2, 3)))
        kv_out.append(kvs3[:, :, :, g])
    return (y_prompt, y_sample, wkv_p, wkv_s, shift_p, shift_s, *kv_out)
```

```python
import functools

import jax
import jax.numpy as jnp
from jax import lax
from jax.experimental import pallas as pl
from jax.experimental.pallas import tpu as pltpu

F32 = jnp.float32
BF16 = jnp.bfloat16

D_MODEL = 1024
HEAD_DIM = 64
N_HEADS = D_MODEL // HEAD_DIM
LANES = 128
N_PAIRS = D_MODEL // LANES
N_GROUPS = 3
WINDOWS = (128, 512, 2048)
DILATIONS = (1, 4, 16)
KEYS_PER_QUERY = 128
ATT_SCALE = HEAD_DIM ** -0.5
ROPE_THETA = 10000.0
NEG_INF = -1e30
GN_EPS = 64e-5
RMS_EPS = 1e-6
PAST_LEN = 2048
DECAY_LOG_SCALE = -0.6065306597126334
WKV_CHUNK = 64
WKV_SMALL_CHUNK = 8
ATTN_UNROLL = 4
ATTN_S_SPLIT = 2
COMPACT_DIL = 4
VMEM_LIMIT = 56 * 1024 * 1024

_HI = lax.Precision.HIGHEST


def _dot(a, b):
    return jnp.dot(a, b, preferred_element_type=F32)


def _dot_nt(a, b):
    return lax.dot_general(a, b, (((1,), (1,)), ((), ())), preferred_element_type=F32)


def _dot_tn(a, b):
    return lax.dot_general(a, b, (((0,), (0,)), ((), ())), preferred_element_type=F32)


def _rms(x, g):
    return x * lax.rsqrt(jnp.mean(x * x, axis=-1, keepdims=True) + RMS_EPS) * g


def _sigmoid(x):
    return 1.0 / (1.0 + jnp.exp(-x))


def _block_ones():
    r = lax.broadcasted_iota(jnp.int32, (LANES, LANES), 0) // HEAD_DIM
    c = lax.broadcasted_iota(jnp.int32, (LANES, LANES), 1) // HEAD_DIM
    return (r == c).astype(BF16)


def _head_sum(x, bo):
    outs = []
    for g in range(x.shape[1] // LANES):
        xg = x[:, g * LANES:(g + 1) * LANES]
        hi = xg.astype(BF16)
        lo = (xg - hi.astype(F32)).astype(BF16)
        outs.append(_dot(hi, bo) + _dot(lo, bo))
    return outs[0] if len(outs) == 1 else jnp.concatenate(outs, axis=1)


def _a_pre_body(has_vres, per_row_shift, seq_len, *refs):
    it = iter(refs)
    x_ref, sp_ref = next(it), next(it)
    vf_ref = next(it) if has_vres else None
    ln_ref, mu_ref, win_ref = next(it), next(it), next(it)
    w0_ref, wla_ref, wlb_ref = next(it), next(it), next(it)
    a0_ref, ala_ref, alb_ref = next(it), next(it), next(it)
    if has_vres:
        v0_ref, vla_ref, vlb_ref = next(it), next(it), next(it)
    kk_ref, ka_ref = next(it), next(it)
    r_o, lw_o, k_o, v_o, kk_o, b_o, z_o, sh_o = [next(it) for _ in range(8)]
    scr = next(it)

    tm = x_ref.shape[0]
    xn = _rms(x_ref[...], ln_ref[...])
    rolled = pltpu.roll(xn, 1, axis=0)
    row = lax.broadcasted_iota(jnp.int32, (tm, 1), 0)
    if per_row_shift:
        prev = jnp.where(row % seq_len == 0, sp_ref[...], rolled)
        for g in range(N_PAIRS):
            lanes = slice(g * LANES, (g + 1) * LANES)
            scr[g] = xn[:, lanes]
            sh_o[:, lanes] = scr[g, pl.ds(seq_len - 1, tm // seq_len, stride=seq_len), :]
    else:
        first = jnp.where(pl.program_id(1) == 0, sp_ref[...], scr[0:1, :])
        prev = jnp.where(row == 0, first, rolled)
        scr[0:1, :] = xn[tm - 1:tm, :]
        sh_o[...] = xn[tm - 1:tm, :]

    dx = prev - xn
    mu = mu_ref[...]

    def mixed(p):
        return xn + mu[p:p + 1, :] * dx

    xv = mixed(2)
    r = _dot(mixed(0).astype(BF16), win_ref[0])
    k = _dot(mixed(1).astype(BF16), win_ref[1])
    v = _dot(xv.astype(BF16), win_ref[2])
    z_o[...] = _dot(mixed(3).astype(BF16), win_ref[3])

    def lora(xm, la_ref, lb_ref, act):
        hdn = _dot(xm.astype(BF16), la_ref[...])
        if act:
            hdn = jnp.tanh(hdn)
        return _dot(hdn.astype(BF16), lb_ref[...])

    wl = w0_ref[...] + lora(mixed(4), wla_ref, wlb_ref, True)
    lw_o[...] = DECAY_LOG_SCALE * _sigmoid(wl)
    a = _sigmoid(a0_ref[...] + lora(mixed(5), ala_ref, alb_ref, False))
    if has_vres:
        v = v + (vf_ref[...] - v) * _sigmoid(v0_ref[...] + lora(xv, vla_ref, vlb_ref, False))

    bo = _block_ones()
    kkp = k * kk_ref[...]
    kk = kkp * lax.rsqrt(jnp.maximum(_head_sum(kkp * kkp, bo), 1e-24))
    r_o[...] = r
    k_o[...] = k * (1.0 + (a - 1.0) * ka_ref[...])
    v_o[...] = v
    kk_o[...] = kk
    b_o[...] = kk * a


def _a_pre(x, shift_prev, v_first, w, l, n_seq, seq_len):
    n = x.shape[0]
    has_vres = v_first is not None
    per_row_shift = seq_len < 256
    if per_row_shift:
        tm = n
        grid = (1, 1)
        sp = jnp.repeat(shift_prev, seq_len, axis=0)
        sp_spec = pl.BlockSpec((tm, D_MODEL), lambda b, i: (0, 0))
        sh_shape = jax.ShapeDtypeStruct((n_seq, D_MODEL), F32)
        sh_spec = pl.BlockSpec((n_seq, D_MODEL), lambda b, i: (0, 0))
        scr = pltpu.VMEM((N_PAIRS, tm, LANES), F32)
        tiles = 1
    else:
        tm = 256
        tiles = seq_len // tm
        grid = (n_seq, tiles)
        sp = shift_prev.reshape(n_seq, 1, D_MODEL)
        sp_spec = pl.BlockSpec((None, 1, D_MODEL), lambda b, i: (b, 0, 0))
        sh_shape = jax.ShapeDtypeStruct((n_seq, 1, D_MODEL), F32)
        sh_spec = pl.BlockSpec((None, 1, D_MODEL), lambda b, i: (b, 0, 0))
        scr = pltpu.VMEM((8, D_MODEL), F32)

    row_spec = pl.BlockSpec((tm, D_MODEL), lambda b, i: (b * tiles + i, 0))

    def of_layer(name, idx):
        arr = w[name]
        return arr, pl.BlockSpec((None,) + arr.shape[1:], lambda b, i: (idx,) + (0,) * (arr.ndim - 1))

    picks = [of_layer(name, l) for name in ('ln', 'mu', 'w_in', 'w0', 'w_la', 'w_lb', 'a0', 'a_la', 'a_lb')]
    if has_vres:
        picks += [of_layer(name, l - 1) for name in ('v0', 'v_la', 'v_lb')]
    picks += [of_layer(name, l) for name in ('k_k', 'k_a')]
    args = [x, sp] + ([v_first] if has_vres else []) + [a for a, _ in picks]
    specs = [row_spec, sp_spec] + ([row_spec] if has_vres else []) + [s for _, s in picks]

    full = jax.ShapeDtypeStruct((n, D_MODEL), F32)
    outs = pl.pallas_call(
        functools.partial(_a_pre_body, has_vres, per_row_shift, seq_len),
        out_shape=[full] * 7 + [sh_shape],
        grid=grid,
        in_specs=specs,
        out_specs=[row_spec] * 7 + [sh_spec],
        scratch_shapes=[scr],
        compiler_params=pltpu.CompilerParams(dimension_semantics=("arbitrary", "arbitrary"),
                                             vmem_limit_bytes=VMEM_LIMIT),
        name="a_pre",
    )(*args)
    return outs[:7], outs[7].reshape(n_seq, D_MODEL)


def _wkv_body(zero_init, L, *refs):
    it = iter(refs)
    r_ref, lw_ref, k_ref, v_ref, kk_ref, b_ref = [next(it) for _ in range(6)]
    st_ref = None if zero_init else next(it)
    y_ref, so_ref = next(it), next(it)
    sbd = next(it)

    n_b = r_ref.shape[0]
    c = pl.program_id(1)
    nc = pl.num_programs(1)
    half = HEAD_DIM
    chains = [(bi, p) for bi in range(n_b) for p in range(N_PAIRS)]

    @pl.when(c == 0)
    def _():
        if zero_init:
            sbd[...] = jnp.zeros(sbd.shape, F32)
        else:
            zeros = jnp.zeros((half, half), F32)
            for bi, p in chains:
                top = jnp.concatenate([st_ref[bi, 2 * p], zeros], axis=1)
                bot = jnp.concatenate([zeros, st_ref[bi, 2 * p + 1]], axis=1)
                sbd[bi, p] = jnp.concatenate([top, bot], axis=0)

    rr = lax.broadcasted_iota(jnp.int32, (L, L), 0)
    cc = lax.broadcasted_iota(jnp.int32, (L, L), 1)
    tri = (rr >= cc).astype(F32)
    lane = lax.broadcasted_iota(jnp.int32, (1, LANES), 1)
    h0 = lane < half
    r2 = lax.broadcasted_iota(jnp.int32, (2 * L, 2 * L), 0)
    c2 = lax.broadcasted_iota(jnp.int32, (2 * L, 2 * L), 1)
    strict = r2 > c2
    incl = r2 >= c2
    eye = (r2 == c2).astype(F32)
    blk = min(16, L)
    diag_blk = (r2 // blk) == (c2 // blk)

    def split(x):
        return jnp.concatenate([jnp.where(h0, x, 0.0), jnp.where(h0, 0.0, x)], axis=0).astype(BF16)

    def mm(a, b):
        return _dot(a.astype(BF16), b.astype(BF16))

    def neumann(a, n_terms):
        x = {ch: eye - a[ch] for ch in chains}
        pw, k = a, 2
        while k < n_terms:
            pw = {ch: mm(pw[ch], pw[ch]) for ch in chains}
            x = {ch: mm(x[ch], eye + pw[ch]) for ch in chains}
            k *= 2
        return x

    lhs4, bts, kls, v2s, bks, e_gls = {}, {}, {}, {}, {}, {}
    for bi in range(n_b):
        lw = lw_ref[bi]
        g_incl = jnp.dot(tri, lw, precision=_HI, preferred_element_type=F32)
        e_g = jnp.exp(g_incl)
        e_ng = jnp.exp(-g_incl)
        e_gl = e_g[L - 1:L, :]
        kt = kk_ref[bi] * jnp.exp(g_incl - lw)
        rt = r_ref[bi] * e_g
        bt = b_ref[bi] * e_ng
        kl = k_ref[bi] * e_ng
        bh = bt * e_gl
        kh = kl * e_gl
        vv = v_ref[bi]
        for p in range(N_PAIRS):
            sl = slice(p * LANES, (p + 1) * LANES)
            lhs4[bi, p] = jnp.concatenate([split(kt[:, sl]), split(rt[:, sl])], axis=0)
            bts[bi, p] = split(bt[:, sl])
            kls[bi, p] = split(kl[:, sl])
            v2s[bi, p] = split(vv[:, sl])
            bks[bi, p] = jnp.concatenate([split(bh[:, sl]), split(kh[:, sl])], axis=0)
            e_gls[bi, p] = e_gl[:, sl]

    wide_ok = (2 * L) % LANES == 0
    if wide_ok:
        aa = {ch: _dot_nt(lhs4[ch], jnp.concatenate([bts[ch], kls[ch]], axis=0)) for ch in chains}
        ab = {ch: aa[ch][:, 0:2 * L] for ch in chains}
        ak = {ch: aa[ch][:, 2 * L:4 * L] for ch in chains}
    else:
        ab = {ch: _dot_nt(lhs4[ch], bts[ch]) for ch in chains}
        ak = {ch: _dot_nt(lhs4[ch], kls[ch]) for ch in chains}
    a_kb = {ch: jnp.where(strict, ab[ch][0:2 * L], 0.0) for ch in chains}
    a_rb = {ch: jnp.where(incl, ab[ch][2 * L:4 * L], 0.0).astype(BF16) for ch in chains}
    a_kk = {ch: jnp.where(strict, ak[ch][0:2 * L], 0.0).astype(BF16) for ch in chains}
    a_rk = {ch: jnp.where(incl, ak[ch][2 * L:4 * L], 0.0).astype(BF16) for ch in chains}
    if L > blk:
        a_d = {ch: jnp.where(diag_blk, a_kb[ch], 0.0) for ch in chains}
        x_d = neumann(a_d, blk)
        nn = {ch: mm(x_d[ch], a_kb[ch] - a_d[ch]) for ch in chains}
        x_n = neumann(nn, L // blk)
        x_f = {ch: mm(x_n[ch], x_d[ch]).astype(BF16) for ch in chains}
    else:
        x_f = {ch: v.astype(BF16) for ch, v in neumann(a_kb, L).items()}

    s_old = {ch: sbd[ch[0], ch[1]] for ch in chains}
    pp = {ch: _dot_nt(lhs4[ch], s_old[ch].astype(BF16)) for ch in chains}
    rhs = {ch: -(pp[ch][0:2 * L] + _dot(a_kk[ch], v2s[ch])) for ch in chains}
    u2 = {ch: _dot(x_f[ch], rhs[ch].astype(BF16)).astype(BF16) for ch in chains}
    for ch in chains:
        bi, p = ch
        uv = jnp.concatenate([u2[ch], v2s[ch]], axis=0)
        if wide_ok:
            ys = pp[ch][2 * L:4 * L] + _dot(jnp.concatenate([a_rb[ch], a_rk[ch]], axis=1), uv)
        else:
            ys = pp[ch][2 * L:4 * L] + _dot(a_rb[ch], u2[ch]) + _dot(a_rk[ch], v2s[ch])
        y_ref[bi, :, p * LANES:(p + 1) * LANES] = ys[0:L] + ys[L:2 * L]
        sbd[bi, p] = s_old[ch] * e_gls[ch] + _dot_tn(uv, bks[ch])

    @pl.when(c == nc - 1)
    def _():
        for bi, p in chains:
            s_p = sbd[bi, p]
            so_ref[bi, 2 * p] = s_p[0:half, 0:half]
            so_ref[bi, 2 * p + 1] = pltpu.roll(s_p[half:2 * half, :], half, axis=1)[:, 0:half]


def _wkv(seqs, states_in, layer, chunk, rows_per_step):
    bsz, t, _ = seqs[0].shape
    zero_init = states_in is None
    nb = rows_per_step
    tok = pl.BlockSpec((nb, chunk, D_MODEL), lambda b, c: (b, c, 0))
    st = pl.BlockSpec((nb, N_HEADS, HEAD_DIM, HEAD_DIM), lambda b, c: (b, 0, 0, 0))
    st_in = pl.BlockSpec((None, nb, N_HEADS, HEAD_DIM, HEAD_DIM), lambda b, c: (layer, b, 0, 0, 0))
    args = list(seqs) + ([] if zero_init else [states_in])
    specs = [tok] * 6 + ([] if zero_init else [st_in])
    return pl.pallas_call(
        functools.partial(_wkv_body, zero_init, chunk),
        out_shape=[jax.ShapeDtypeStruct((bsz, t, D_MODEL), F32),
                   jax.ShapeDtypeStruct((bsz, N_HEADS, HEAD_DIM, HEAD_DIM), F32)],
        grid=(bsz // nb, t // chunk),
        in_specs=specs,
        out_specs=[tok, st],
        scratch_shapes=[pltpu.VMEM((nb, N_PAIRS, LANES, LANES), F32)],
        compiler_params=pltpu.CompilerParams(dimension_semantics=("arbitrary", "arbitrary"),
                                             vmem_limit_bytes=VMEM_LIMIT),
        name="wkv",
    )(*args)


def _a_post_body(y_ref, r_ref, k_ref, v_ref, z_ref, x_ref, rk_ref, gw_ref, gb_ref, wo_ref, o_ref):
    bo = _block_ones()
    y = y_ref[...]
    inv_c = 1.0 / HEAD_DIM
    yc = y - _head_sum(y, bo) * inv_c
    var = _head_sum(yc * yc, bo) * inv_c
    yn = yc * lax.rsqrt(var + GN_EPS) * gw_ref[...] + gb_ref[...]
    yn = yn + _head_sum(r_ref[...] * k_ref[...] * rk_ref[...], bo) * v_ref[...]
    z = z_ref[...]
    gated = yn * (z * _sigmoid(z))
    o_ref[...] = x_ref[...] + _dot(gated.astype(BF16), wo_ref[...])


def _a_post(y, r, k, v, z, x, w, l):
    n = x.shape[0]
    tm = 512
    row = pl.BlockSpec((tm, D_MODEL), lambda i: (i, 0))
    names = ('r_k', 'gn_w', 'gn_b', 'w_out')
    picks = [pl.BlockSpec((None,) + w[nm].shape[1:], lambda i, nd=w[nm].ndim: (l,) + (0,) * (nd - 1)) for nm in names]
    return pl.pallas_call(
        _a_post_body,
        out_shape=jax.ShapeDtypeStruct((n, D_MODEL), F32),
        grid=(n // tm,),
        in_specs=[row] * 6 + picks,
        out_specs=row,
        compiler_params=pltpu.CompilerParams(dimension_semantics=("arbitrary",), vmem_limit_bytes=VMEM_LIMIT),
        name="a_post",
    )(y, r, k, v, z, x, *[w[nm] for nm in names])


def _rope_tile(t, cos_t, sin_t):
    lane = lax.broadcasted_iota(jnp.int32, (1, LANES), 1)
    first_half = (lane % HEAD_DIM) < (HEAD_DIM // 2)
    outs = []
    for g in range(t.shape[1] // LANES):
        xg = t[:, g * LANES:(g + 1) * LANES]
        partner = jnp.where(first_half, pltpu.roll(xg, LANES - HEAD_DIM // 2, axis=1),
                            pltpu.roll(xg, HEAD_DIM // 2, axis=1))
        outs.append(xg * cos_t + partner * sin_t)
    return jnp.concatenate(outs, axis=1)


def _proj_body(n_rope, pair_major, x_ref, ln_ref, w_ref, cos_ref, sin_ref, o_ref):
    xn = _rms(x_ref[...], ln_ref[...]).astype(BF16)
    for j in range(w_ref.shape[1] // D_MODEL):
        cols = slice(j * D_MODEL, (j + 1) * D_MODEL)
        t = _dot(xn, w_ref[:, cols])
        if j < n_rope:
            t = _rope_tile(t, cos_ref[...], sin_ref[...])
        if pair_major:
            for p in range(N_PAIRS):
                o_ref[j * N_PAIRS + p] = t[:, p * LANES:(p + 1) * LANES]
        else:
            o_ref[:, cols] = t


def _proj(x, ln, w, cos_t, sin_t, n_rope, seqs=None):
    n = x.shape[0]
    n_out = w.shape[1]
    tm = min(256, n)
    n_tab = cos_t.shape[0] // tm
    tab = pl.BlockSpec((tm, LANES), lambda i: (i % n_tab, 0))
    if seqs is None:
        out_shape = jax.ShapeDtypeStruct((n, n_out), F32)
        out_spec = pl.BlockSpec((tm, n_out), lambda i: (i, 0))
    else:
        bsz, seq_len = seqs
        tiles = seq_len // tm
        out_shape = jax.ShapeDtypeStruct((bsz, n_out // LANES, seq_len, LANES), F32)
        out_spec = pl.BlockSpec((None, n_out // LANES, tm, LANES), lambda i: (i // tiles, 0, i % tiles, 0))
    return pl.pallas_call(
        functools.partial(_proj_body, n_rope, seqs is not None),
        out_shape=out_shape,
        grid=(n // tm,),
        in_specs=[pl.BlockSpec((tm, D_MODEL), lambda i: (i, 0)),
                  pl.BlockSpec((1, D_MODEL), lambda i: (0, 0)),
                  pl.BlockSpec((D_MODEL, n_out), lambda i: (0, 0)),
                  tab, tab],
        out_specs=out_spec,
        compiler_params=pltpu.CompilerParams(dimension_semantics=("arbitrary",), vmem_limit_bytes=VMEM_LIMIT),
        name="proj",
    )(x, ln, w, cos_t, sin_t)


def _attn_prompt_body(seq_len, q0, q1, q2, k0, k1, k2, v0, v1, v2, o_ref, m_s, l_s, a_s):
    nq = KEYS_PER_QUERY
    lane = lax.broadcasted_iota(jnp.int32, (1, LANES), 1)
    h0 = lane < HEAD_DIM
    qi = lax.broadcasted_iota(jnp.int32, (nq, nq), 0)
    ki = lax.broadcasted_iota(jnp.int32, (nq, nq), 1)
    cur_mask = ki <= qi
    prev_mask = ki > qi

    heads = (h0, jnp.logical_not(h0))

    def run_blocks(g, q_ref, k_ref, v_ref, d, blocks):
        rows, qs, kbs, vbs, masks = [], [], [], [], []
        cur = []
        for c, nb, prev in blocks:
            rw = pl.ds(c + d * nq * nb, nq, stride=d)
            cur.append((k_ref[rw, :].astype(BF16), v_ref[rw, :].astype(BF16)))
            rows.append(rw)
        for i, (c, nb, prev) in enumerate(blocks):
            kb, vb = cur[i]
            mask = cur_mask
            if prev is not None:
                if prev is True and i > 0:
                    kp, vp = cur[i - 1]
                else:
                    pb = nb - 1 if prev is True else jnp.maximum(nb - 1, 0)
                    prow = pl.ds(c + d * nq * pb, nq, stride=d)
                    kp, vp = k_ref[prow, :].astype(BF16), v_ref[prow, :].astype(BF16)
                kb = jnp.concatenate([kp, kb], axis=0)
                vb = jnp.concatenate([vp, vb], axis=0)
                pm = prev_mask if prev is True else jnp.logical_and(prev_mask, prev)
                mask = jnp.concatenate([pm, cur_mask], axis=1)
            rw = rows[i]
            qs.append(q_ref[rw, :] * ATT_SCALE)
            kbs.append(kb)
            vbs.append(vb)
            masks.append(mask)
        n = len(blocks)
        pairs = [(i, h) for i in range(n) for h in range(2)]
        s = {(i, h): _dot_nt(jnp.where(heads[h], qs[i], 0.0).astype(BF16), kbs[i]) for i, h in pairs}
        s = {ih: jnp.where(masks[ih[0]], s[ih], NEG_INF) for ih in pairs}
        mx = {ih: jnp.max(s[ih], axis=-1, keepdims=True) for ih in pairs}
        pr = {ih: jnp.exp(s[ih] - mx[ih]) for ih in pairs}
        ls = {ih: jnp.sum(pr[ih], axis=-1, keepdims=True) for ih in pairs}
        acc = {(i, h): _dot(pr[i, h].astype(BF16), vbs[i]) for i, h in pairs}
        for i in range(n):
            m_b = jnp.where(h0, mx[i, 0], mx[i, 1])
            l_b = jnp.where(h0, ls[i, 0], ls[i, 1])
            a_b = jnp.where(h0, acc[i, 0], acc[i, 1])
            if g > 0:
                m_o, l_o, a_o = m_s[rows[i], :], l_s[rows[i], :], a_s[rows[i], :]
                m_n = jnp.maximum(m_o, m_b)
                w_o = jnp.exp(m_o - m_n)
                w_b = jnp.exp(m_b - m_n)
                m_b = m_n
                l_b = w_o * l_o + w_b * l_b
                a_b = w_o * a_o + w_b * a_b
            if g == N_GROUPS - 1:
                o_ref[rows[i], :] = a_b / l_b
            else:
                m_s[rows[i], :] = m_b
                l_s[rows[i], :] = l_b
                a_s[rows[i], :] = a_b

    u = ATTN_UNROLL
    for g, (q_ref, k_ref, v_ref) in enumerate(((q0, k0, v0), (q1, k1, v1), (q2, k2, v2))):
        d = DILATIONS[g]
        n_blk = seq_len // (d * nq)
        if n_blk == 1:
            def classes(i, carry, g=g, q_ref=q_ref, k_ref=k_ref, v_ref=v_ref, d=d):
                run_blocks(g, q_ref, k_ref, v_ref, d, [(i * u + j, 0, None) for j in range(u)])
                return carry
            lax.fori_loop(0, d // u, classes, 0)
        elif n_blk == u:
            def one_class(c, carry, g=g, q_ref=q_ref, k_ref=k_ref, v_ref=v_ref, d=d):
                run_blocks(g, q_ref, k_ref, v_ref, d, [(c, nb, True if nb else None) for nb in range(u)])
                return carry
            lax.fori_loop(0, d, one_class, 0)
        else:
            def blocks(i, carry, g=g, q_ref=q_ref, k_ref=k_ref, v_ref=v_ref, d=d):
                run_blocks(g, q_ref, k_ref, v_ref, d,
                           [(0, i * u + j, (i > 0) if j == 0 else True) for j in range(u)])
                return carry
            assert d == 1 and n_blk % u == 0
            lax.fori_loop(0, n_blk // u, blocks, 0)


def _attn_prompt(proj, kv):
    bsz, _, seq_len, _ = proj.shape

    def col(base):
        return pl.BlockSpec((None, None, seq_len, LANES), lambda b, p: (b, base + p, 0, 0))

    q_specs = [col(g * N_PAIRS) for g in range(N_GROUPS)]
    k_specs = [col(g * N_PAIRS) for g in range(N_GROUPS)]
    v_specs = [col((N_GROUPS + g) * N_PAIRS) for g in range(N_GROUPS)]
    return pl.pallas_call(
        functools.partial(_attn_prompt_body, seq_len),
        out_shape=jax.ShapeDtypeStruct((bsz, N_PAIRS, seq_len, LANES), F32),
        grid=(bsz, N_PAIRS),
        in_specs=q_specs + k_specs + v_specs,
        out_specs=col(0),
        scratch_shapes=[pltpu.VMEM((seq_len, LANES), F32)] * 3,
        compiler_params=pltpu.CompilerParams(dimension_semantics=("arbitrary", "arbitrary"),
                                             vmem_limit_bytes=VMEM_LIMIT),
        name="attn_prompt",
    )(proj, proj, proj, kv, kv, kv, kv, kv, kv)


def _attn_sample_body(n_new, wide_part, n_split, compacted, emit, *refs):
    if not wide_part:
        q_ref, kvn_ref, c0_ref, c1_ref, m_o, l_o, acc_o = refs
    elif emit:
        q_ref, c2_ref, m_i, l_i, acc_i, o_ref, cc_ref, m_s, l_s, acc_s = refs
    else:
        q_ref, c2_ref, m_i, l_i, acc_i, o_ref, m_s, l_s, acc_s = refs
    rows = n_new * N_HEADS
    rid = lax.broadcasted_iota(jnp.int32, (rows, D_MODEL), 0)
    cid = lax.broadcasted_iota(jnp.int32, (rows, D_MODEL), 1)
    own_head = (rid % N_HEADS) == (cid // HEAD_DIM)
    jrow = lax.broadcasted_iota(jnp.int32, (rows, 1), 0) // N_HEADS

    def q_rows(g):
        qg = q_ref[:, g * D_MODEL:(g + 1) * D_MODEL] * ATT_SCALE
        full = jnp.concatenate([jnp.broadcast_to(qg[j:j + 1], (N_HEADS, D_MODEL)) for j in range(n_new)], axis=0)
        return jnp.where(own_head, full, 0.0)

    def absorb_cache(state, d, qb, kt, vt, first_row):
        s = _dot(qb.astype(BF16), kt)
        r = first_row + lax.broadcasted_iota(jnp.int32, (1, s.shape[1]), 1)
        s = jnp.where((r % d == jrow % d) & (r >= jrow + d), s, NEG_INF)
        m_c = jnp.max(s, axis=-1, keepdims=True)
        if state is None:
            m_n, alpha = m_c, None
        else:
            m_n = jnp.maximum(state[0], m_c)
            alpha = jnp.exp(state[0] - m_n)
        e = jnp.exp(s - m_n)
        l_c = jnp.sum(e, axis=-1, keepdims=True)
        a_c = _dot_nt(e.astype(BF16), vt)
        if state is None:
            return m_n, l_c, a_c
        return m_n, alpha * state[1] + l_c, alpha * state[2] + a_c

    def absorb_new(state, g, qb):
        d = DILATIONS[g]
        k_new = kvn_ref[:, g * D_MODEL:(g + 1) * D_MODEL]
        v_new = kvn_ref[:, (N_GROUPS + g) * D_MODEL:(N_GROUPS + g + 1) * D_MODEL]
        s = []
        for i in range(n_new):
            ok = (jrow >= i) & ((jrow - i) % d == 0)
            s.append(jnp.where(ok, jnp.sum(qb * k_new[i:i + 1], axis=-1, keepdims=True), NEG_INF))
        m_n = functools.reduce(jnp.maximum, s, state[0])
        alpha = jnp.exp(state[0] - m_n)
        l_n, a_n = alpha * state[1], alpha * state[2]
        for i in range(n_new):
            e = jnp.exp(s[i] - m_n)
            l_n = l_n + e
            a_n = a_n + e * v_new[i:i + 1]
        return m_n, l_n, a_n

    if not wide_part:
        st = None
        for g, c_ref in ((0, c0_ref), (1, c1_ref)):
            qb = q_rows(g)
            st = absorb_cache(st, DILATIONS[g], qb, c_ref[0].astype(BF16), c_ref[1].astype(BF16), 0)
            st = absorb_new(st, g, qb)
        st = absorb_new(st, N_GROUPS - 1, q_rows(N_GROUPS - 1))
        m_o[...] = jnp.broadcast_to(st[0], m_o.shape)
        l_o[...] = jnp.broadcast_to(st[1], l_o.shape)
        acc_o[...] = st[2]
        return

    w = pl.program_id(1)

    @pl.when(w == 0)
    def _():
        m_s[...] = m_i[:, 0:1]
        l_s[...] = l_i[:, 0:1]
        acc_s[...] = acc_i[...]

    wide = c2_ref.shape[2]
    kt, vt = c2_ref[0].astype(BF16), c2_ref[1].astype(BF16)
    lane_dil = COMPACT_DIL if compacted else DILATIONS[N_GROUPS - 1]
    m_s[...], l_s[...], acc_s[...] = absorb_cache((m_s[...], l_s[...], acc_s[...]), lane_dil,
                                                  q_rows(N_GROUPS - 1), kt, vt, w * wide)
    if emit:
        slab = LANES * COMPACT_DIL
        rr = lax.broadcasted_iota(jnp.int32, (slab, LANES), 0)
        cc = lax.broadcasted_iota(jnp.int32, (slab, LANES), 1)
        sel = (rr == DILATIONS[N_GROUPS - 1] * (cc // COMPACT_DIL) + cc % COMPACT_DIL).astype(BF16)
        for i, x in enumerate((kt, vt)):
            for k in range(wide // slab):
                cc_ref[i, :, k * LANES:(k + 1) * LANES] = _dot(x[:, k * slab:(k + 1) * slab], sel).astype(BF16)

    @pl.when(w == n_split - 1)
    def _():
        o = jnp.where(own_head, acc_s[...] / l_s[...], 0.0)
        o_ref[...] = jnp.sum(o.reshape(n_new, N_HEADS, D_MODEL), axis=1)


def _attn_sample(proj, kv, caches_t, bsz, n_new, emit):
    proj3 = proj.reshape(bsz, n_new, proj.shape[1])
    kv3 = kv.reshape(bsz, n_new, kv.shape[1])
    last = N_GROUPS - 1
    w_last = caches_t[last].shape[3]
    compacted = w_last != WINDOWS[last]
    assert n_new <= COMPACT_DIL and not (compacted and emit)
    assert w_last == (WINDOWS[last] // DILATIONS[last] * COMPACT_DIL if compacted else WINDOWS[last])
    n_split = 1 if compacted else ATTN_S_SPLIT
    for g in range(last):
        assert caches_t[g].shape[3] == WINDOWS[g]
    rows = n_new * N_HEADS

    stat = jax.ShapeDtypeStruct((bsz, rows, LANES), F32)
    stat_spec = pl.BlockSpec((None, rows, LANES), lambda b: (b, 0, 0))
    acc_spec = pl.BlockSpec((None, rows, D_MODEL), lambda b: (b, 0, 0))
    m0, l0, acc0 = pl.pallas_call(
        functools.partial(_attn_sample_body, n_new, False, 1, False, False),
        out_shape=[stat, stat, jax.ShapeDtypeStruct((bsz, rows, D_MODEL), F32)],
        grid=(bsz,),
        in_specs=[pl.BlockSpec((None, n_new, proj.shape[1]), lambda b: (b, 0, 0)),
                  pl.BlockSpec((None, n_new, kv.shape[1]), lambda b: (b, 0, 0))]
                 + [pl.BlockSpec((None, 2, D_MODEL, WINDOWS[g]), lambda b: (b, 0, 0, 0)) for g in range(last)],
        out_specs=[stat_spec, stat_spec, acc_spec],
        compiler_params=pltpu.CompilerParams(dimension_semantics=("arbitrary",), vmem_limit_bytes=VMEM_LIMIT),
        name="attn_sample_narrow",
    )(proj3, kv3, *caches_t[:last])

    out_shape = [jax.ShapeDtypeStruct((bsz, n_new, D_MODEL), F32)]
    out_specs = [pl.BlockSpec((None, n_new, D_MODEL), lambda b, w: (b, 0, 0))]
    if emit:
        w_c = w_last // DILATIONS[last] * COMPACT_DIL
        out_shape.append(jax.ShapeDtypeStruct((bsz, 2, D_MODEL, w_c), BF16))
        out_specs.append(pl.BlockSpec((None, 2, D_MODEL, w_c // n_split), lambda b, w: (b, 0, 0, w)))
    stat_in = pl.BlockSpec((None, rows, LANES), lambda b, w: (b, 0, 0))
    outs = pl.pallas_call(
        functools.partial(_attn_sample_body, n_new, True, n_split, compacted, emit),
        out_shape=out_shape,
        grid=(bsz, n_split),
        in_specs=[pl.BlockSpec((None, n_new, proj.shape[1]), lambda b, w: (b, 0, 0)),
                  pl.BlockSpec((None, 2, D_MODEL, w_last // n_split), lambda b, w: (b, 0, 0, w)),
                  stat_in, stat_in, pl.BlockSpec((None, rows, D_MODEL), lambda b, w: (b, 0, 0))],
        out_specs=out_specs,
        scratch_shapes=[pltpu.VMEM((rows, 1), F32), pltpu.VMEM((rows, 1), F32), pltpu.VMEM((rows, D_MODEL), F32)],
        compiler_params=pltpu.CompilerParams(dimension_semantics=("arbitrary", "arbitrary"),
                                             vmem_limit_bytes=VMEM_LIMIT),
        name="attn_sample_wide",
    )(proj3, caches_t[last], m0, l0, acc0)
    o = outs[0].reshape(bsz * n_new, D_MODEL)
    return (o, outs[1]) if emit else o


def _wide(ref):
    if len(ref.shape) == 2:
        return ref[...]
    return jnp.concatenate([ref[p] for p in range(ref.shape[0])], axis=1)


def _b_post_body(final, *refs):
    if final:
        o_ref, z_ref, x_ref, wo_ref, fl_ref, out_ref = refs
    else:
        o_ref, z_ref, x_ref, wo_ref, out_ref = refs
    z = _wide(z_ref)
    gated = _wide(o_ref) * (z * _sigmoid(z))
    h = x_ref[...] + _dot(gated.astype(BF16), wo_ref[...])
    out_ref[...] = _rms(h, fl_ref[...]) if final else h


def _b_post(o, proj, x, w_out, final_ln):
    n = x.shape[0]
    tm = min(512, n)
    final = final_ln is not None
    row = pl.BlockSpec((tm, D_MODEL), lambda i: (i, 0))
    if o.ndim == 2:
        o_spec, z_spec = row, pl.BlockSpec((tm, D_MODEL), lambda i: (i, N_GROUPS))
    else:
        tiles = o.shape[2] // tm
        o_spec = pl.BlockSpec((None, N_PAIRS, tm, LANES), lambda i: (i // tiles, 0, i % tiles, 0))
        z_spec = pl.BlockSpec((None, N_PAIRS, tm, LANES), lambda i: (i // tiles, N_GROUPS, i % tiles, 0))
    specs = [o_spec, z_spec, row, pl.BlockSpec((D_MODEL, D_MODEL), lambda i: (0, 0))]
    args = [o, proj, x, w_out]
    if final:
        specs.append(pl.BlockSpec((1, D_MODEL), lambda i: (0, 0)))
        args.append(final_ln)
    return pl.pallas_call(
        functools.partial(_b_post_body, final),
        out_shape=jax.ShapeDtypeStruct((n, D_MODEL), F32),
        grid=(n // tm,),
        in_specs=specs,
        out_specs=row,
        compiler_params=pltpu.CompilerParams(dimension_semantics=("arbitrary",), vmem_limit_bytes=VMEM_LIMIT),
        name="b_post",
    )(*args)


def _kv_tail_t_body(x_ref, o_ref):
    o_ref[...] = _wide(x_ref).T


def _kv_tail_t(kv, g, wc):
    bsz, _, seq_len, _ = kv.shape
    tk = min(512, wc)
    first = (seq_len - wc) // tk
    return pl.pallas_call(
        _kv_tail_t_body,
        out_shape=jax.ShapeDtypeStruct((bsz, 2, D_MODEL, wc), F32),
        grid=(bsz, 2, wc // tk),
        in_specs=[pl.BlockSpec((None, N_PAIRS, tk, LANES),
                               lambda b, kv_i, i: (b, kv_i * N_GROUPS + g, first + i, 0))],
        out_specs=pl.BlockSpec((None, None, D_MODEL, tk), lambda b, kv_i, i: (b, kv_i, 0, i)),
        compiler_params=pltpu.CompilerParams(dimension_semantics=("arbitrary",) * 3, vmem_limit_bytes=VMEM_LIMIT),
        name="kv_tail_t",
    )(kv)


def _rope_tables(pos):
    half = HEAD_DIM // 2
    inv = ROPE_THETA ** (-jnp.arange(half, dtype=F32) / half)
    ang = pos.astype(F32)[:, None] * inv[None, :]
    cos, sin = jnp.cos(ang), jnp.sin(ang)
    return jnp.concatenate([cos, cos, cos, cos], axis=1), jnp.concatenate([-sin, sin, -sin, sin], axis=1)


def _rwkv_stack(x, shift_init, wkv_init, n_seq, seq_len, w, n_layers):
    shifts, states = [], []
    v_first = None
    if seq_len % WKV_CHUNK == 0:
        chunk, rows_per_step, pad = WKV_CHUNK, 2, 0
    else:
        chunk, rows_per_step = WKV_SMALL_CHUNK, 2
        pad = (-seq_len) % chunk
    for l in range(n_layers):
        (r, lw, k, v, kk, b, z), sh = _a_pre(x, shift_init[l], v_first, w, l, n_seq, seq_len)
        if l == 0:
            v_first = v
        seqs = [t.reshape(n_seq, seq_len, D_MODEL) for t in (r, lw, k, v, kk, b)]
        if pad:
            seqs = [jnp.pad(t, ((0, 0), (0, pad), (0, 0))) for t in seqs]
        y, st = _wkv(seqs, wkv_init, l, chunk, rows_per_step)
        y = y[:, :seq_len].reshape(n_seq * seq_len, D_MODEL)
        x = _a_post(y, r, k, v, z, x, w, l)
        shifts.append(sh)
        states.append(st)
    return x, jnp.stack(shifts), jnp.stack(states)


def kernel(x_prompt, x_sample, state_wkv, state_shift, cache_kv_g0, cache_kv_g1, cache_kv_g2, a_ln, a_mu, a_w_in, a_w0, a_w_lora_a, a_w_lora_b, a_a0, a_a_lora_a, a_a_lora_b, a_v0, a_v_lora_a, a_v_lora_b, a_k_k, a_k_a, a_r_k, a_gn_w, a_gn_b, a_w_out, kv_ln, w_kv, b_ln, b_w_in, b_w_out, final_ln):
    bp, s, _ = x_prompt.shape
    bs, t, _ = x_sample.shape
    n_a = a_ln.shape[0]
    n_b = b_ln.shape[0]

    def row(vec):
        return vec.reshape(1, D_MODEL)

    def rows(stacked):
        return stacked.reshape(stacked.shape[0], 1, D_MODEL)

    w_a = dict(ln=rows(a_ln), mu=a_mu, w_in=a_w_in.astype(BF16), w0=rows(a_w0),
               w_la=a_w_lora_a.astype(BF16), w_lb=a_w_lora_b.astype(BF16), a0=rows(a_a0),
               a_la=a_a_lora_a.astype(BF16), a_lb=a_a_lora_b.astype(BF16),
               v0=rows(a_v0), v_la=a_v_lora_a.astype(BF16), v_lb=a_v_lora_b.astype(BF16),
               k_k=rows(a_k_k), k_a=rows(a_k_a), r_k=rows(a_r_k), gn_w=rows(a_gn_w),
               gn_b=rows(a_gn_b), w_out=a_w_out.astype(BF16))

    xp = x_prompt.reshape(bp * s, D_MODEL)
    xs = x_sample.reshape(bs * t, D_MODEL)
    zero_shift = jnp.zeros((n_a, bp, D_MODEL), F32)
    hp, shift_p, wkv_p = _rwkv_stack(xp, zero_shift, None, bp, s, w_a, n_a)
    hs, shift_s, wkv_s = _rwkv_stack(xs, state_shift, state_wkv, bs, t, w_a, n_a)

    cos_p, sin_p = _rope_tables(jnp.arange(s, dtype=jnp.int32))
    cos_s, sin_s = _rope_tables(jnp.tile(PAST_LEN + jnp.arange(t, dtype=jnp.int32), bs))
    w_kv_b = w_kv.astype(BF16)
    kv_ln_r = row(kv_ln)
    kvp = _proj(hp, kv_ln_r, w_kv_b, cos_p, sin_p, N_GROUPS, (bp, s))
    kvs = _proj(hs, kv_ln_r, w_kv_b, cos_s, sin_s, N_GROUPS)

    caches = tuple(jnp.transpose(c, (0, 2, 3, 4, 1)).reshape(bs, 2, D_MODEL, c.shape[1])
                   for c in (cache_kv_g0, cache_kv_g1, cache_kv_g2))
    for l in range(n_b):
        w_in = b_w_in[l].astype(BF16)
        w_out = b_w_out[l].astype(BF16)
        fl = row(final_ln) if l == n_b - 1 else None
        pj = _proj(hp, row(b_ln[l]), w_in, cos_p, sin_p, N_GROUPS, (bp, s))
        hp = _b_post(_attn_prompt(pj, kvp), pj, hp, w_out, fl)
        pj = _proj(hs, row(b_ln[l]), w_in, cos_s, sin_s, N_GROUPS)
        if l == 0 and n_b > 1:
            o_s, compact = _attn_sample(pj, kvs, caches, bs, t, True)
            caches = caches[:-1] + (compact,)
        else:
            o_s = _attn_sample(pj, kvs, caches, bs, t, False)
        hs = _b_post(o_s, pj, hs, w_out, fl)

    y_prompt = hp.reshape(bp, s, D_MODEL)
    y_sample = hs.reshape(bs, t, D_MODEL)

    kvs3 = kvs.reshape(bs, t, 2, N_GROUPS, N_HEADS, HEAD_DIM)
    kv_out = []
    for g in range(N_GROUPS):
        wc = min(WINDOWS[g], s)
        rows_minor = _kv_tail_t(kvp, g, wc).reshape(bp, 2, N_HEADS, HEAD_DIM, wc)
        kv_out.append(jnp.transpose(rows_minor, (0, 4, 1, 2, 3)))
        kv_out.append(kvs3[:, :, :, g])
    return (y_prompt, y_sample, wkv_p, wkv_s, shift_p, shift_s, *kv_out)
```

```python
import functools

import jax
import jax.numpy as jnp
from jax import lax
from jax.experimental import pallas as pl
from jax.experimental.pallas import tpu as pltpu

F32 = jnp.float32
BF16 = jnp.bfloat16

D_MODEL = 1024
HEAD_DIM = 64
N_HEADS = D_MODEL // HEAD_DIM
LANES = 128
N_PAIRS = D_MODEL // LANES
N_GROUPS = 3
WINDOWS = (128, 512, 2048)
DILATIONS = (1, 4, 16)
KEYS_PER_QUERY = 128
ATT_SCALE = HEAD_DIM ** -0.5
ROPE_THETA = 10000.0
NEG_INF = -1e30
GN_EPS = 64e-5
RMS_EPS = 1e-6
PAST_LEN = 2048
DECAY_LOG_SCALE = -0.6065306597126334
WKV_CHUNK = 64
WKV_SMALL_CHUNK = 8
ATTN_UNROLL = 4
ATTN_S_SPLIT = 2
COMPACT_DIL = 4
VMEM_LIMIT = 56 * 1024 * 1024

_HI = lax.Precision.HIGHEST


def _dot(a, b):
    return jnp.dot(a, b, preferred_element_type=F32)


def _dot_nt(a, b):
    return lax.dot_general(a, b, (((1,), (1,)), ((), ())), preferred_element_type=F32)


def _dot_tn(a, b):
    return lax.dot_general(a, b, (((0,), (0,)), ((), ())), preferred_element_type=F32)


def _rms(x, g):
    return x * lax.rsqrt(jnp.mean(x * x, axis=-1, keepdims=True) + RMS_EPS) * g


def _sigmoid(x):
    return 1.0 / (1.0 + jnp.exp(-x))


def _block_ones():
    r = lax.broadcasted_iota(jnp.int32, (LANES, LANES), 0) // HEAD_DIM
    c = lax.broadcasted_iota(jnp.int32, (LANES, LANES), 1) // HEAD_DIM
    return (r == c).astype(BF16)


def _head_sum(x, bo):
    outs = []
    for g in range(x.shape[1] // LANES):
        xg = x[:, g * LANES:(g + 1) * LANES]
        hi = xg.astype(BF16)
        lo = (xg - hi.astype(F32)).astype(BF16)
        outs.append(_dot(hi, bo) + _dot(lo, bo))
    return outs[0] if len(outs) == 1 else jnp.concatenate(outs, axis=1)


def _a_pre_body(has_vres, per_row_shift, seq_len, *refs):
    it = iter(refs)
    x_ref, sp_ref = next(it), next(it)
    vf_ref = next(it) if has_vres else None
    ln_ref, mu_ref, win_ref = next(it), next(it), next(it)
    w0_ref, wla_ref, wlb_ref = next(it), next(it), next(it)
    a0_ref, ala_ref, alb_ref = next(it), next(it), next(it)
    if has_vres:
        v0_ref, vla_ref, vlb_ref = next(it), next(it), next(it)
    kk_ref, ka_ref = next(it), next(it)
    r_o, lw_o, k_o, v_o, kk_o, b_o, z_o, sh_o = [next(it) for _ in range(8)]
    scr = next(it)

    tm = x_ref.shape[0]
    xn = _rms(x_ref[...], ln_ref[...])
    rolled = pltpu.roll(xn, 1, axis=0)
    row = lax.broadcasted_iota(jnp.int32, (tm, 1), 0)
    if per_row_shift:
        prev = jnp.where(row % seq_len == 0, sp_ref[...], rolled)
        for g in range(N_PAIRS):
            lanes = slice(g * LANES, (g + 1) * LANES)
            scr[g] = xn[:, lanes]
            sh_o[:, lanes] = scr[g, pl.ds(seq_len - 1, tm // seq_len, stride=seq_len), :]
    else:
        first = jnp.where(pl.program_id(1) == 0, sp_ref[...], scr[0:1, :])
        prev = jnp.where(row == 0, first, rolled)
        scr[0:1, :] = xn[tm - 1:tm, :]
        sh_o[...] = xn[tm - 1:tm, :]

    dx = prev - xn
    mu = mu_ref[...]

    def mixed(p):
        return xn + mu[p:p + 1, :] * dx

    xv = mixed(2)
    r = _dot(mixed(0).astype(BF16), win_ref[0])
    k = _dot(mixed(1).astype(BF16), win_ref[1])
    v = _dot(xv.astype(BF16), win_ref[2])
    z_o[...] = _dot(mixed(3).astype(BF16), win_ref[3])

    def lora(xm, la_ref, lb_ref, act):
        hdn = _dot(xm.astype(BF16), la_ref[...])
        if act:
            hdn = jnp.tanh(hdn)
        return _dot(hdn.astype(BF16), lb_ref[...])

    wl = w0_ref[...] + lora(mixed(4), wla_ref, wlb_ref, True)
    lw_o[...] = DECAY_LOG_SCALE * _sigmoid(wl)
    a = _sigmoid(a0_ref[...] + lora(mixed(5), ala_ref, alb_ref, False))
    if has_vres:
        v = v + (vf_ref[...] - v) * _sigmoid(v0_ref[...] + lora(xv, vla_ref, vlb_ref, False))

    bo = _block_ones()
    kkp = k * kk_ref[...]
    kk = kkp * lax.rsqrt(jnp.maximum(_head_sum(kkp * kkp, bo), 1e-24))
    r_o[...] = r
    k_o[...] = k * (1.0 + (a - 1.0) * ka_ref[...])
    v_o[...] = v
    kk_o[...] = kk
    b_o[...] = kk * a


def _a_pre(x, shift_prev, v_first, w, l, n_seq, seq_len):
    n = x.shape[0]
    has_vres = v_first is not None
    per_row_shift = seq_len < 256
    if per_row_shift:
        tm = n
        grid = (1, 1)
        sp = jnp.repeat(shift_prev, seq_len, axis=0)
        sp_spec = pl.BlockSpec((tm, D_MODEL), lambda b, i: (0, 0))
        sh_shape = jax.ShapeDtypeStruct((n_seq, D_MODEL), F32)
        sh_spec = pl.BlockSpec((n_seq, D_MODEL), lambda b, i: (0, 0))
        scr = pltpu.VMEM((N_PAIRS, tm, LANES), F32)
        tiles = 1
    else:
        tm = 256
        tiles = seq_len // tm
        grid = (n_seq, tiles)
        sp = shift_prev.reshape(n_seq, 1, D_MODEL)
        sp_spec = pl.BlockSpec((None, 1, D_MODEL), lambda b, i: (b, 0, 0))
        sh_shape = jax.ShapeDtypeStruct((n_seq, 1, D_MODEL), F32)
        sh_spec = pl.BlockSpec((None, 1, D_MODEL), lambda b, i: (b, 0, 0))
        scr = pltpu.VMEM((8, D_MODEL), F32)

    row_spec = pl.BlockSpec((tm, D_MODEL), lambda b, i: (b * tiles + i, 0))

    def of_layer(name, idx):
        arr = w[name]
        return arr, pl.BlockSpec((None,) + arr.shape[1:], lambda b, i: (idx,) + (0,) * (arr.ndim - 1))

    picks = [of_layer(name, l) for name in ('ln', 'mu', 'w_in', 'w0', 'w_la', 'w_lb', 'a0', 'a_la', 'a_lb')]
    if has_vres:
        picks += [of_layer(name, l - 1) for name in ('v0', 'v_la', 'v_lb')]
    picks += [of_layer(name, l) for name in ('k_k', 'k_a')]
    args = [x, sp] + ([v_first] if has_vres else []) + [a for a, _ in picks]
    specs = [row_spec, sp_spec] + ([row_spec] if has_vres else []) + [s for _, s in picks]

    full = jax.ShapeDtypeStruct((n, D_MODEL), F32)
    outs = pl.pallas_call(
        functools.partial(_a_pre_body, has_vres, per_row_shift, seq_len),
        out_shape=[full] * 7 + [sh_shape],
        grid=grid,
        in_specs=specs,
        out_specs=[row_spec] * 7 + [sh_spec],
        scratch_shapes=[scr],
        compiler_params=pltpu.CompilerParams(dimension_semantics=("arbitrary", "arbitrary"),
                                             vmem_limit_bytes=VMEM_LIMIT),
        name="a_pre",
    )(*args)
    return outs[:7], outs[7].reshape(n_seq, D_MODEL)


def _wkv_body(zero_init, L, *refs):
    it = iter(refs)
    r_ref, lw_ref, k_ref, v_ref, kk_ref, b_ref = [next(it) for _ in range(6)]
    st_ref = None if zero_init else next(it)
    y_ref, so_ref = next(it), next(it)
    sbd = next(it)

    n_b = r_ref.shape[0]
    c = pl.program_id(1)
    nc = pl.num_programs(1)
    half = HEAD_DIM
    chains = [(bi, p) for bi in range(n_b) for p in range(N_PAIRS)]

    @pl.when(c == 0)
    def _():
        if zero_init:
            sbd[...] = jnp.zeros(sbd.shape, F32)
        else:
            zeros = jnp.zeros((half, half), F32)
            for bi, p in chains:
                top = jnp.concatenate([st_ref[bi, 2 * p], zeros], axis=1)
                bot = jnp.concatenate([zeros, st_ref[bi, 2 * p + 1]], axis=1)
                sbd[bi, p] = jnp.concatenate([top, bot], axis=0)

    rr = lax.broadcasted_iota(jnp.int32, (L, L), 0)
    cc = lax.broadcasted_iota(jnp.int32, (L, L), 1)
    tri = (rr >= cc).astype(F32)
    lane = lax.broadcasted_iota(jnp.int32, (1, LANES), 1)
    h0 = lane < half
    r2 = lax.broadcasted_iota(jnp.int32, (2 * L, 2 * L), 0)
    c2 = lax.broadcasted_iota(jnp.int32, (2 * L, 2 * L), 1)
    strict = r2 > c2
    incl = r2 >= c2
    eye = (r2 == c2).astype(F32)
    blk = min(16, L)
    diag_blk = (r2 // blk) == (c2 // blk)

    def split(x):
        return jnp.concatenate([jnp.where(h0, x, 0.0), jnp.where(h0, 0.0, x)], axis=0).astype(BF16)

    def mm(a, b):
        return _dot(a.astype(BF16), b.astype(BF16))

    def neumann(a, n_terms):
        x = {ch: eye - a[ch] for ch in chains}
        pw, k = a, 2
        while k < n_terms:
            pw = {ch: mm(pw[ch], pw[ch]) for ch in chains}
            x = {ch: mm(x[ch], eye + pw[ch]) for ch in chains}
            k *= 2
        return x

    lhs4, bts, kls, v2s, bks, e_gls = {}, {}, {}, {}, {}, {}
    for bi in range(n_b):
        lw = lw_ref[bi]
        g_incl = jnp.dot(tri, lw, precision=_HI, preferred_element_type=F32)
        e_g = jnp.exp(g_incl)
        e_ng = jnp.exp(-g_incl)
        e_gl = e_g[L - 1:L, :]
        kt = kk_ref[bi] * jnp.exp(g_incl - lw)
        rt = r_ref[bi] * e_g
        bt = b_ref[bi] * e_ng
        kl = k_ref[bi] * e_ng
        bh = bt * e_gl
        kh = kl * e_gl
        vv = v_ref[bi]
        for p in range(N_PAIRS):
            sl = slice(p * LANES, (p + 1) * LANES)
            lhs4[bi, p] = jnp.concatenate([split(kt[:, sl]), split(rt[:, sl])], axis=0)
            bts[bi, p] = split(bt[:, sl])
            kls[bi, p] = split(kl[:, sl])
            v2s[bi, p] = split(vv[:, sl])
            bks[bi, p] = jnp.concatenate([split(bh[:, sl]), split(kh[:, sl])], axis=0)
            e_gls[bi, p] = e_gl[:, sl]

    wide_ok = (2 * L) % LANES == 0
    if wide_ok:
        aa = {ch: _dot_nt(lhs4[ch], jnp.concatenate([bts[ch], kls[ch]], axis=0)) for ch in chains}
        ab = {ch: aa[ch][:, 0:2 * L] for ch in chains}
        ak = {ch: aa[ch][:, 2 * L:4 * L] for ch in chains}
    else:
        ab = {ch: _dot_nt(lhs4[ch], bts[ch]) for ch in chains}
        ak = {ch: _dot_nt(lhs4[ch], kls[ch]) for ch in chains}
    a_kb = {ch: jnp.where(strict, ab[ch][0:2 * L], 0.0) for ch in chains}
    a_rb = {ch: jnp.where(incl, ab[ch][2 * L:4 * L], 0.0).astype(BF16) for ch in chains}
    a_kk = {ch: jnp.where(strict, ak[ch][0:2 * L], 0.0).astype(BF16) for ch in chains}
    a_rk = {ch: jnp.where(incl, ak[ch][2 * L:4 * L], 0.0).astype(BF16) for ch in chains}
    if L > blk:
        a_d = {ch: jnp.where(diag_blk, a_kb[ch], 0.0) for ch in chains}
        x_d = neumann(a_d, blk)
        nn = {ch: mm(x_d[ch], a_kb[ch] - a_d[ch]) for ch in chains}
        x_n = neumann(nn, L // blk)
        x_f = {ch: mm(x_n[ch], x_d[ch]).astype(BF16) for ch in chains}
    else:
        x_f = {ch: v.astype(BF16) for ch, v in neumann(a_kb, L).items()}

    s_old = {ch: sbd[ch[0], ch[1]] for ch in chains}
    pp = {ch: _dot_nt(lhs4[ch], s_old[ch].astype(BF16)) for ch in chains}
    rhs = {ch: -(pp[ch][0:2 * L] + _dot(a_kk[ch], v2s[ch])) for ch in chains}
    u2 = {ch: _dot(x_f[ch], rhs[ch].astype(BF16)).astype(BF16) for ch in chains}
    for ch in chains:
        bi, p = ch
        uv = jnp.concatenate([u2[ch], v2s[ch]], axis=0)
        if wide_ok:
            ys = pp[ch][2 * L:4 * L] + _dot(jnp.concatenate([a_rb[ch], a_rk[ch]], axis=1), uv)
        else:
            ys = pp[ch][2 * L:4 * L] + _dot(a_rb[ch], u2[ch]) + _dot(a_rk[ch], v2s[ch])
        y_ref[bi, :, p * LANES:(p + 1) * LANES] = ys[0:L] + ys[L:2 * L]
        sbd[bi, p] = s_old[ch] * e_gls[ch] + _dot_tn(uv, bks[ch])

    @pl.when(c == nc - 1)
    def _():
        for bi, p in chains:
            s_p = sbd[bi, p]
            so_ref[bi, 2 * p] = s_p[0:half, 0:half]
            so_ref[bi, 2 * p + 1] = pltpu.roll(s_p[half:2 * half, :], half, axis=1)[:, 0:half]


def _wkv(seqs, state0, chunk, rows_per_step):
    bsz, t, _ = seqs[0].shape
    zero_init = state0 is None
    nb = rows_per_step
    tok = pl.BlockSpec((nb, chunk, D_MODEL), lambda b, c: (b, c, 0))
    st = pl.BlockSpec((nb, N_HEADS, HEAD_DIM, HEAD_DIM), lambda b, c: (b, 0, 0, 0))
    args = list(seqs) + ([] if zero_init else [state0])
    specs = [tok] * 6 + ([] if zero_init else [st])
    return pl.pallas_call(
        functools.partial(_wkv_body, zero_init, chunk),
        out_shape=[jax.ShapeDtypeStruct((bsz, t, D_MODEL), F32),
                   jax.ShapeDtypeStruct((bsz, N_HEADS, HEAD_DIM, HEAD_DIM), F32)],
        grid=(bsz // nb, t // chunk),
        in_specs=specs,
        out_specs=[tok, st],
        scratch_shapes=[pltpu.VMEM((nb, N_PAIRS, LANES, LANES), F32)],
        compiler_params=pltpu.CompilerParams(dimension_semantics=("arbitrary", "arbitrary"),
                                             vmem_limit_bytes=VMEM_LIMIT),
        name="wkv",
    )(*args)


def _a_post_body(y_ref, r_ref, k_ref, v_ref, z_ref, x_ref, rk_ref, gw_ref, gb_ref, wo_ref, o_ref):
    bo = _block_ones()
    y = y_ref[...]
    inv_c = 1.0 / HEAD_DIM
    yc = y - _head_sum(y, bo) * inv_c
    var = _head_sum(yc * yc, bo) * inv_c
    yn = yc * lax.rsqrt(var + GN_EPS) * gw_ref[...] + gb_ref[...]
    yn = yn + _head_sum(r_ref[...] * k_ref[...] * rk_ref[...], bo) * v_ref[...]
    z = z_ref[...]
    gated = yn * (z * _sigmoid(z))
    o_ref[...] = x_ref[...] + _dot(gated.astype(BF16), wo_ref[...])


def _a_post(y, r, k, v, z, x, w, l):
    n = x.shape[0]
    tm = 512
    row = pl.BlockSpec((tm, D_MODEL), lambda i: (i, 0))
    names = ('r_k', 'gn_w', 'gn_b', 'w_out')
    picks = [pl.BlockSpec((None,) + w[nm].shape[1:], lambda i, nd=w[nm].ndim: (l,) + (0,) * (nd - 1)) for nm in names]
    return pl.pallas_call(
        _a_post_body,
        out_shape=jax.ShapeDtypeStruct((n, D_MODEL), F32),
        grid=(n // tm,),
        in_specs=[row] * 6 + picks,
        out_specs=row,
        compiler_params=pltpu.CompilerParams(dimension_semantics=("arbitrary",), vmem_limit_bytes=VMEM_LIMIT),
        name="a_post",
    )(y, r, k, v, z, x, *[w[nm] for nm in names])


def _rope_tile(t, cos_t, sin_t):
    lane = lax.broadcasted_iota(jnp.int32, (1, LANES), 1)
    first_half = (lane % HEAD_DIM) < (HEAD_DIM // 2)
    outs = []
    for g in range(t.shape[1] // LANES):
        xg = t[:, g * LANES:(g + 1) * LANES]
        partner = jnp.where(first_half, pltpu.roll(xg, LANES - HEAD_DIM // 2, axis=1),
                            pltpu.roll(xg, HEAD_DIM // 2, axis=1))
        outs.append(xg * cos_t + partner * sin_t)
    return jnp.concatenate(outs, axis=1)


def _proj_body(n_rope, pair_major, n_tails, x_ref, ln_ref, w_ref, cos_ref, sin_ref, o_ref, *tail_refs):
    xn = _rms(x_ref[...], ln_ref[...]).astype(BF16)
    for j in range(w_ref.shape[1] // D_MODEL):
        cols = slice(j * D_MODEL, (j + 1) * D_MODEL)
        t = _dot(xn, w_ref[:, cols])
        if j < n_rope:
            t = _rope_tile(t, cos_ref[...], sin_ref[...])
        if pair_major:
            for p in range(N_PAIRS):
                o_ref[j * N_PAIRS + p] = t[:, p * LANES:(p + 1) * LANES]
        else:
            o_ref[:, cols] = t
        if n_tails:
            tail = tail_refs[j % n_tails]
            keep = tail.shape[2]
            tail[j // n_tails] = t[t.shape[0] - keep:, :].T


def _proj(x, ln, w, cos_t, sin_t, n_rope, seqs=None, tails=None):
    n = x.shape[0]
    n_out = w.shape[1]
    tm = min(256, n)
    n_tab = cos_t.shape[0] // tm
    tab = pl.BlockSpec((tm, LANES), lambda i: (i % n_tab, 0))
    if seqs is None:
        out_shape = [jax.ShapeDtypeStruct((n, n_out), F32)]
        out_spec = [pl.BlockSpec((tm, n_out), lambda i: (i, 0))]
    else:
        bsz, seq_len = seqs
        tiles = seq_len // tm
        out_shape = [jax.ShapeDtypeStruct((bsz, n_out // LANES, seq_len, LANES), F32)]
        out_spec = [pl.BlockSpec((None, n_out // LANES, tm, LANES), lambda i: (i // tiles, 0, i % tiles, 0))]
    for rows in tails or ():
        width = min(rows, tm)
        first = tiles - rows // width
        out_shape.append(jax.ShapeDtypeStruct((bsz, 2, D_MODEL, rows), F32))
        out_spec.append(pl.BlockSpec((None, 2, D_MODEL, width),
                                     lambda i, first=first: (i // tiles, 0, 0, jnp.maximum(i % tiles - first, 0))))
    outs = pl.pallas_call(
        functools.partial(_proj_body, n_rope, seqs is not None, len(tails or ())),
        out_shape=out_shape,
        grid=(n // tm,),
        in_specs=[pl.BlockSpec((tm, D_MODEL), lambda i: (i, 0)),
                  pl.BlockSpec((1, D_MODEL), lambda i: (0, 0)),
                  pl.BlockSpec((D_MODEL, n_out), lambda i: (0, 0)),
                  tab, tab],
        out_specs=out_spec,
        compiler_params=pltpu.CompilerParams(dimension_semantics=("arbitrary",), vmem_limit_bytes=VMEM_LIMIT),
        name="proj",
    )(x, ln, w, cos_t, sin_t)
    return outs if tails else outs[0]


def _attn_prompt_body(seq_len, q0, q1, q2, k0, k1, k2, v0, v1, v2, o_ref, m_s, l_s, a_s):
    nq = KEYS_PER_QUERY
    lane = lax.broadcasted_iota(jnp.int32, (1, LANES), 1)
    h0 = lane < HEAD_DIM
    qi = lax.broadcasted_iota(jnp.int32, (nq, nq), 0)
    ki = lax.broadcasted_iota(jnp.int32, (nq, nq), 1)
    cur_mask = ki <= qi
    prev_mask = ki > qi

    heads = (h0, jnp.logical_not(h0))

    def run_blocks(g, q_ref, k_ref, v_ref, d, blocks):
        rows, qs, kbs, vbs, masks = [], [], [], [], []
        cur = []
        for c, nb, prev in blocks:
            rw = pl.ds(c + d * nq * nb, nq, stride=d)
            cur.append((k_ref[rw, :].astype(BF16), v_ref[rw, :].astype(BF16)))
            rows.append(rw)
        for i, (c, nb, prev) in enumerate(blocks):
            kb, vb = cur[i]
            mask = cur_mask
            if prev is not None:
                if prev is True and i > 0:
                    kp, vp = cur[i - 1]
                else:
                    pb = nb - 1 if prev is True else jnp.maximum(nb - 1, 0)
                    prow = pl.ds(c + d * nq * pb, nq, stride=d)
                    kp, vp = k_ref[prow, :].astype(BF16), v_ref[prow, :].astype(BF16)
                kb = jnp.concatenate([kp, kb], axis=0)
                vb = jnp.concatenate([vp, vb], axis=0)
                pm = prev_mask if prev is True else jnp.logical_and(prev_mask, prev)
                mask = jnp.concatenate([pm, cur_mask], axis=1)
            rw = rows[i]
            qs.append(q_ref[rw, :] * ATT_SCALE)
            kbs.append(kb)
            vbs.append(vb)
            masks.append(mask)
        n = len(blocks)
        pairs = [(i, h) for i in range(n) for h in range(2)]
        s = {(i, h): _dot_nt(jnp.where(heads[h], qs[i], 0.0).astype(BF16), kbs[i]) for i, h in pairs}
        s = {ih: jnp.where(masks[ih[0]], s[ih], NEG_INF) for ih in pairs}
        mx = {ih: jnp.max(s[ih], axis=-1, keepdims=True) for ih in pairs}
        pr = {ih: jnp.exp(s[ih] - mx[ih]) for ih in pairs}
        ls = {ih: jnp.sum(pr[ih], axis=-1, keepdims=True) for ih in pairs}
        acc = {(i, h): _dot(pr[i, h].astype(BF16), vbs[i]) for i, h in pairs}
        for i in range(n):
            m_b = jnp.where(h0, mx[i, 0], mx[i, 1])
            l_b = jnp.where(h0, ls[i, 0], ls[i, 1])
            a_b = jnp.where(h0, acc[i, 0], acc[i, 1])
            if g > 0:
                m_o, l_o, a_o = m_s[rows[i], :], l_s[rows[i], :], a_s[rows[i], :]
                m_n = jnp.maximum(m_o, m_b)
                w_o = jnp.exp(m_o - m_n)
                w_b = jnp.exp(m_b - m_n)
                m_b = m_n
                l_b = w_o * l_o + w_b * l_b
                a_b = w_o * a_o + w_b * a_b
            if g == N_GROUPS - 1:
                o_ref[rows[i], :] = a_b / l_b
            else:
                m_s[rows[i], :] = m_b
                l_s[rows[i], :] = l_b
                a_s[rows[i], :] = a_b

    u = ATTN_UNROLL
    for g, (q_ref, k_ref, v_ref) in enumerate(((q0, k0, v0), (q1, k1, v1), (q2, k2, v2))):
        d = DILATIONS[g]
        n_blk = seq_len // (d * nq)
        if n_blk == 1:
            def classes(i, carry, g=g, q_ref=q_ref, k_ref=k_ref, v_ref=v_ref, d=d):
                run_blocks(g, q_ref, k_ref, v_ref, d, [(i * u + j, 0, None) for j in range(u)])
                return carry
            lax.fori_loop(0, d // u, classes, 0)
        elif n_blk == u:
            def one_class(c, carry, g=g, q_ref=q_ref, k_ref=k_ref, v_ref=v_ref, d=d):
                run_blocks(g, q_ref, k_ref, v_ref, d, [(c, nb, True if nb else None) for nb in range(u)])
                return carry
            lax.fori_loop(0, d, one_class, 0)
        else:
            def blocks(i, carry, g=g, q_ref=q_ref, k_ref=k_ref, v_ref=v_ref, d=d):
                run_blocks(g, q_ref, k_ref, v_ref, d,
                           [(0, i * u + j, (i > 0) if j == 0 else True) for j in range(u)])
                return carry
            assert d == 1 and n_blk % u == 0
            lax.fori_loop(0, n_blk // u, blocks, 0)


def _attn_prompt(proj, kv):
    bsz, _, seq_len, _ = proj.shape

    def col(base):
        return pl.BlockSpec((None, None, seq_len, LANES), lambda b, p: (b, base + p, 0, 0))

    q_specs = [col(g * N_PAIRS) for g in range(N_GROUPS)]
    k_specs = [col(g * N_PAIRS) for g in range(N_GROUPS)]
    v_specs = [col((N_GROUPS + g) * N_PAIRS) for g in range(N_GROUPS)]
    return pl.pallas_call(
        functools.partial(_attn_prompt_body, seq_len),
        out_shape=jax.ShapeDtypeStruct((bsz, N_PAIRS, seq_len, LANES), F32),
        grid=(bsz, N_PAIRS),
        in_specs=q_specs + k_specs + v_specs,
        out_specs=col(0),
        scratch_shapes=[pltpu.VMEM((seq_len, LANES), F32)] * 3,
        compiler_params=pltpu.CompilerParams(dimension_semantics=("arbitrary", "arbitrary"),
                                             vmem_limit_bytes=VMEM_LIMIT),
        name="attn_prompt",
    )(proj, proj, proj, kv, kv, kv, kv, kv, kv)


def _attn_sample_body(n_new, n_split, compacted, emit, *refs):
    if emit:
        q_ref, kvn_ref, c0_ref, c1_ref, c2_ref, o_ref, cc_ref, m_s, l_s, acc_s = refs
    else:
        q_ref, kvn_ref, c0_ref, c1_ref, c2_ref, o_ref, m_s, l_s, acc_s = refs
    w = pl.program_id(1)
    rows = n_new * N_HEADS
    rid = lax.broadcasted_iota(jnp.int32, (rows, D_MODEL), 0)
    cid = lax.broadcasted_iota(jnp.int32, (rows, D_MODEL), 1)
    own_head = (rid % N_HEADS) == (cid // HEAD_DIM)
    jrow = lax.broadcasted_iota(jnp.int32, (rows, 1), 0) // N_HEADS

    def q_rows(g):
        qg = q_ref[:, g * D_MODEL:(g + 1) * D_MODEL] * ATT_SCALE
        full = jnp.concatenate([jnp.broadcast_to(qg[j:j + 1], (N_HEADS, D_MODEL)) for j in range(n_new)], axis=0)
        return jnp.where(own_head, full, 0.0)

    def absorb_cache(state, d, qb, kt, vt, first_row):
        s = _dot(qb.astype(BF16), kt)
        r = first_row + lax.broadcasted_iota(jnp.int32, (1, s.shape[1]), 1)
        s = jnp.where((r % d == jrow % d) & (r >= jrow + d), s, NEG_INF)
        m_c = jnp.max(s, axis=-1, keepdims=True)
        if state is None:
            m_n, alpha = m_c, None
        else:
            m_n = jnp.maximum(state[0], m_c)
            alpha = jnp.exp(state[0] - m_n)
        e = jnp.exp(s - m_n)
        l_c = jnp.sum(e, axis=-1, keepdims=True)
        a_c = _dot_nt(e.astype(BF16), vt)
        if state is None:
            return m_n, l_c, a_c
        return m_n, alpha * state[1] + l_c, alpha * state[2] + a_c

    def absorb_new(state, g, qb):
        d = DILATIONS[g]
        k_new = kvn_ref[:, g * D_MODEL:(g + 1) * D_MODEL]
        v_new = kvn_ref[:, (N_GROUPS + g) * D_MODEL:(N_GROUPS + g + 1) * D_MODEL]
        s = []
        for i in range(n_new):
            ok = (jrow >= i) & ((jrow - i) % d == 0)
            s.append(jnp.where(ok, jnp.sum(qb * k_new[i:i + 1], axis=-1, keepdims=True), NEG_INF))
        m_n = functools.reduce(jnp.maximum, s, state[0])
        alpha = jnp.exp(state[0] - m_n)
        l_n, a_n = alpha * state[1], alpha * state[2]
        for i in range(n_new):
            e = jnp.exp(s[i] - m_n)
            l_n = l_n + e
            a_n = a_n + e * v_new[i:i + 1]
        return m_n, l_n, a_n

    @pl.when(w == 0)
    def _():
        st = None
        for g, c_ref in ((0, c0_ref), (1, c1_ref)):
            qb = q_rows(g)
            st = absorb_cache(st, DILATIONS[g], qb, c_ref[0].astype(BF16), c_ref[1].astype(BF16), 0)
            st = absorb_new(st, g, qb)
        m_s[...], l_s[...], acc_s[...] = absorb_new(st, N_GROUPS - 1, q_rows(N_GROUPS - 1))

    wide = c2_ref.shape[2]
    kt, vt = c2_ref[0].astype(BF16), c2_ref[1].astype(BF16)
    lane_dil = COMPACT_DIL if compacted else DILATIONS[N_GROUPS - 1]
    m_s[...], l_s[...], acc_s[...] = absorb_cache((m_s[...], l_s[...], acc_s[...]), lane_dil,
                                                  q_rows(N_GROUPS - 1), kt, vt, w * wide)
    if emit:
        slab = LANES * COMPACT_DIL
        rr = lax.broadcasted_iota(jnp.int32, (slab, LANES), 0)
        cc = lax.broadcasted_iota(jnp.int32, (slab, LANES), 1)
        sel = (rr == DILATIONS[N_GROUPS - 1] * (cc // COMPACT_DIL) + cc % COMPACT_DIL).astype(BF16)
        for i, x in enumerate((kt, vt)):
            for k in range(wide // slab):
                cc_ref[i, :, k * LANES:(k + 1) * LANES] = _dot(x[:, k * slab:(k + 1) * slab], sel).astype(BF16)

    @pl.when(w == n_split - 1)
    def _():
        o = jnp.where(own_head, acc_s[...] / l_s[...], 0.0)
        o_ref[...] = jnp.sum(o.reshape(n_new, N_HEADS, D_MODEL), axis=1)


def _attn_sample(proj, kv, caches_t, bsz, n_new, emit):
    proj3 = proj.reshape(bsz, n_new, proj.shape[1])
    kv3 = kv.reshape(bsz, n_new, kv.shape[1])
    last = N_GROUPS - 1
    w_last = caches_t[last].shape[3]
    compacted = w_last != WINDOWS[last]
    assert n_new <= COMPACT_DIL and not (compacted and emit)
    assert w_last == (WINDOWS[last] // DILATIONS[last] * COMPACT_DIL if compacted else WINDOWS[last])
    n_split = 1 if compacted else ATTN_S_SPLIT
    specs = [pl.BlockSpec((None, 2, D_MODEL, WINDOWS[g]), lambda b, w: (b, 0, 0, 0)) for g in range(last)]
    specs.append(pl.BlockSpec((None, 2, D_MODEL, w_last // n_split), lambda b, w: (b, 0, 0, w)))
    for g in range(last):
        assert caches_t[g].shape[3] == WINDOWS[g]
    rows = n_new * N_HEADS
    out_shape = [jax.ShapeDtypeStruct((bsz, n_new, D_MODEL), F32)]
    out_specs = [pl.BlockSpec((None, n_new, D_MODEL), lambda b, w: (b, 0, 0))]
    if emit:
        w_c = w_last // DILATIONS[last] * COMPACT_DIL
        out_shape.append(jax.ShapeDtypeStruct((bsz, 2, D_MODEL, w_c), BF16))
        out_specs.append(pl.BlockSpec((None, 2, D_MODEL, w_c // n_split), lambda b, w: (b, 0, 0, w)))
    outs = pl.pallas_call(
        functools.partial(_attn_sample_body, n_new, n_split, compacted, emit),
        out_shape=out_shape,
        grid=(bsz, n_split),
        in_specs=[pl.BlockSpec((None, n_new, proj.shape[1]), lambda b, w: (b, 0, 0)),
                  pl.BlockSpec((None, n_new, kv.shape[1]), lambda b, w: (b, 0, 0))] + specs,
        out_specs=out_specs,
        scratch_shapes=[pltpu.VMEM((rows, 1), F32), pltpu.VMEM((rows, 1), F32), pltpu.VMEM((rows, D_MODEL), F32)],
        compiler_params=pltpu.CompilerParams(dimension_semantics=("arbitrary", "arbitrary"),
                                             vmem_limit_bytes=VMEM_LIMIT),
        name="attn_sample",
    )(proj3, kv3, *caches_t)
    o = outs[0].reshape(bsz * n_new, D_MODEL)
    return (o, outs[1]) if emit else o


def _wide(ref):
    if len(ref.shape) == 2:
        return ref[...]
    return jnp.concatenate([ref[p] for p in range(ref.shape[0])], axis=1)


def _b_post_body(final, *refs):
    if final:
        o_ref, z_ref, x_ref, wo_ref, fl_ref, out_ref = refs
    else:
        o_ref, z_ref, x_ref, wo_ref, out_ref = refs
    z = _wide(z_ref)
    gated = _wide(o_ref) * (z * _sigmoid(z))
    h = x_ref[...] + _dot(gated.astype(BF16), wo_ref[...])
    out_ref[...] = _rms(h, fl_ref[...]) if final else h


def _b_post(o, proj, x, w_out, final_ln):
    n = x.shape[0]
    tm = min(512, n)
    final = final_ln is not None
    row = pl.BlockSpec((tm, D_MODEL), lambda i: (i, 0))
    if o.ndim == 2:
        o_spec, z_spec = row, pl.BlockSpec((tm, D_MODEL), lambda i: (i, N_GROUPS))
    else:
        tiles = o.shape[2] // tm
        o_spec = pl.BlockSpec((None, N_PAIRS, tm, LANES), lambda i: (i // tiles, 0, i % tiles, 0))
        z_spec = pl.BlockSpec((None, N_PAIRS, tm, LANES), lambda i: (i // tiles, N_GROUPS, i % tiles, 0))
    specs = [o_spec, z_spec, row, pl.BlockSpec((D_MODEL, D_MODEL), lambda i: (0, 0))]
    args = [o, proj, x, w_out]
    if final:
        specs.append(pl.BlockSpec((1, D_MODEL), lambda i: (0, 0)))
        args.append(final_ln)
    return pl.pallas_call(
        functools.partial(_b_post_body, final),
        out_shape=jax.ShapeDtypeStruct((n, D_MODEL), F32),
        grid=(n // tm,),
        in_specs=specs,
        out_specs=row,
        compiler_params=pltpu.CompilerParams(dimension_semantics=("arbitrary",), vmem_limit_bytes=VMEM_LIMIT),
        name="b_post",
    )(*args)


def _rope_tables(pos):
    half = HEAD_DIM // 2
    inv = ROPE_THETA ** (-jnp.arange(half, dtype=F32) / half)
    ang = pos.astype(F32)[:, None] * inv[None, :]
    cos, sin = jnp.cos(ang), jnp.sin(ang)
    return jnp.concatenate([cos, cos, cos, cos], axis=1), jnp.concatenate([-sin, sin, -sin, sin], axis=1)


def _rwkv_stack(x, shift_init, wkv_init, n_seq, seq_len, w, n_layers):
    shifts, states = [], []
    v_first = None
    if seq_len % WKV_CHUNK == 0:
        chunk, rows_per_step, pad = WKV_CHUNK, 4, 0
    else:
        chunk, rows_per_step = WKV_SMALL_CHUNK, 8
        pad = (-seq_len) % chunk
    for l in range(n_layers):
        (r, lw, k, v, kk, b, z), sh = _a_pre(x, shift_init[l], v_first, w, l, n_seq, seq_len)
        if l == 0:
            v_first = v
        seqs = [t.reshape(n_seq, seq_len, D_MODEL) for t in (r, lw, k, v, kk, b)]
        if pad:
            seqs = [jnp.pad(t, ((0, 0), (0, pad), (0, 0))) for t in seqs]
        y, st = _wkv(seqs, None if wkv_init is None else wkv_init[l], chunk, rows_per_step)
        y = y[:, :seq_len].reshape(n_seq * seq_len, D_MODEL)
        x = _a_post(y, r, k, v, z, x, w, l)
        shifts.append(sh)
        states.append(st)
    return x, jnp.stack(shifts), jnp.stack(states)


def kernel(x_prompt, x_sample, state_wkv, state_shift, cache_kv_g0, cache_kv_g1, cache_kv_g2, a_ln, a_mu, a_w_in, a_w0, a_w_lora_a, a_w_lora_b, a_a0, a_a_lora_a, a_a_lora_b, a_v0, a_v_lora_a, a_v_lora_b, a_k_k, a_k_a, a_r_k, a_gn_w, a_gn_b, a_w_out, kv_ln, w_kv, b_ln, b_w_in, b_w_out, final_ln):
    bp, s, _ = x_prompt.shape
    bs, t, _ = x_sample.shape
    n_a = a_ln.shape[0]
    n_b = b_ln.shape[0]

    def row(vec):
        return vec.reshape(1, D_MODEL)

    def rows(stacked):
        return stacked.reshape(stacked.shape[0], 1, D_MODEL)

    w_a = dict(ln=rows(a_ln), mu=a_mu, w_in=a_w_in.astype(BF16), w0=rows(a_w0),
               w_la=a_w_lora_a.astype(BF16), w_lb=a_w_lora_b.astype(BF16), a0=rows(a_a0),
               a_la=a_a_lora_a.astype(BF16), a_lb=a_a_lora_b.astype(BF16),
               v0=rows(a_v0), v_la=a_v_lora_a.astype(BF16), v_lb=a_v_lora_b.astype(BF16),
               k_k=rows(a_k_k), k_a=rows(a_k_a), r_k=rows(a_r_k), gn_w=rows(a_gn_w),
               gn_b=rows(a_gn_b), w_out=a_w_out.astype(BF16))

    xp = x_prompt.reshape(bp * s, D_MODEL)
    xs = x_sample.reshape(bs * t, D_MODEL)
    zero_shift = jnp.zeros((n_a, bp, D_MODEL), F32)
    hp, shift_p, wkv_p = _rwkv_stack(xp, zero_shift, None, bp, s, w_a, n_a)
    hs, shift_s, wkv_s = _rwkv_stack(xs, state_shift, state_wkv, bs, t, w_a, n_a)

    cos_p, sin_p = _rope_tables(jnp.arange(s, dtype=jnp.int32))
    cos_s, sin_s = _rope_tables(jnp.tile(PAST_LEN + jnp.arange(t, dtype=jnp.int32), bs))
    w_kv_b = w_kv.astype(BF16)
    kv_ln_r = row(kv_ln)
    tail_rows = [min(WINDOWS[g], s) for g in range(N_GROUPS)]
    kvp, *kv_tails = _proj(hp, kv_ln_r, w_kv_b, cos_p, sin_p, N_GROUPS, (bp, s), tail_rows)
    kvs = _proj(hs, kv_ln_r, w_kv_b, cos_s, sin_s, N_GROUPS)

    caches = tuple(jnp.transpose(c, (0, 2, 3, 4, 1)).reshape(bs, 2, D_MODEL, c.shape[1])
                   for c in (cache_kv_g0, cache_kv_g1, cache_kv_g2))
    for l in range(n_b):
        w_in = b_w_in[l].astype(BF16)
        w_out = b_w_out[l].astype(BF16)
        fl = row(final_ln) if l == n_b - 1 else None
        pj = _proj(hp, row(b_ln[l]), w_in, cos_p, sin_p, N_GROUPS, (bp, s))
        hp = _b_post(_attn_prompt(pj, kvp), pj, hp, w_out, fl)
        pj = _proj(hs, row(b_ln[l]), w_in, cos_s, sin_s, N_GROUPS)
        if l == 0 and n_b > 1:
            o_s, compact = _attn_sample(pj, kvs, caches, bs, t, True)
            caches = caches[:-1] + (compact,)
        else:
            o_s = _attn_sample(pj, kvs, caches, bs, t, False)
        hs = _b_post(o_s, pj, hs, w_out, fl)

    y_prompt = hp.reshape(bp, s, D_MODEL)
    y_sample = hs.reshape(bs, t, D_MODEL)

    kvs3 = kvs.reshape(bs, t, 2, N_GROUPS, N_HEADS, HEAD_DIM)
    kv_out = []
    for g in range(N_GROUPS):
        rows_minor = kv_tails[g].reshape(bp, 2, N_HEADS, HEAD_DIM, tail_rows[g])
        kv_out.append(jnp.transpose(rows_minor, (0, 4, 1, 2, 3)))
        kv_out.append(kvs3[:, :, :, g])
    return (y_prompt, y_sample, wkv_p, wkv_s, shift_p, shift_s, *kv_out)
```

```python
import functools

import jax
import jax.numpy as jnp
from jax import lax
from jax.experimental import pallas as pl
from jax.experimental.pallas import tpu as pltpu

F32 = jnp.float32
BF16 = jnp.bfloat16

D_MODEL = 1024
HEAD_DIM = 64
N_HEADS = D_MODEL // HEAD_DIM
LANES = 128
N_PAIRS = D_MODEL // LANES
N_GROUPS = 3
WINDOWS = (128, 512, 2048)
DILATIONS = (1, 4, 16)
KEYS_PER_QUERY = 128
ATT_SCALE = HEAD_DIM ** -0.5
ROPE_THETA = 10000.0
NEG_INF = -1e30
GN_EPS = 64e-5
RMS_EPS = 1e-6
PAST_LEN = 2048
DECAY_LOG_SCALE = -0.6065306597126334
WKV_CHUNK = 64
WKV_SMALL_CHUNK = 8
ATTN_UNROLL = 4
ATTN_S_SPLIT = 1
COMPACT_DIL = 4
VMEM_LIMIT = 56 * 1024 * 1024

_HI = lax.Precision.HIGHEST


def _dot(a, b):
    return jnp.dot(a, b, preferred_element_type=F32)


def _dot_nt(a, b):
    return lax.dot_general(a, b, (((1,), (1,)), ((), ())), preferred_element_type=F32)


def _dot_tn(a, b):
    return lax.dot_general(a, b, (((0,), (0,)), ((), ())), preferred_element_type=F32)


def _rms(x, g):
    return x * lax.rsqrt(jnp.mean(x * x, axis=-1, keepdims=True) + RMS_EPS) * g


def _sigmoid(x):
    return 1.0 / (1.0 + jnp.exp(-x))


def _block_ones():
    r = lax.broadcasted_iota(jnp.int32, (LANES, LANES), 0) // HEAD_DIM
    c = lax.broadcasted_iota(jnp.int32, (LANES, LANES), 1) // HEAD_DIM
    return (r == c).astype(BF16)


def _head_sum(x, bo):
    outs = []
    for g in range(x.shape[1] // LANES):
        xg = x[:, g * LANES:(g + 1) * LANES]
        hi = xg.astype(BF16)
        lo = (xg - hi.astype(F32)).astype(BF16)
        outs.append(_dot(hi, bo) + _dot(lo, bo))
    return outs[0] if len(outs) == 1 else jnp.concatenate(outs, axis=1)


def _a_pre_body(has_vres, per_row_shift, seq_len, *refs):
    it = iter(refs)
    x_ref, sp_ref = next(it), next(it)
    vf_ref = next(it) if has_vres else None
    ln_ref, mu_ref, win_ref = next(it), next(it), next(it)
    w0_ref, wla_ref, wlb_ref = next(it), next(it), next(it)
    a0_ref, ala_ref, alb_ref = next(it), next(it), next(it)
    if has_vres:
        v0_ref, vla_ref, vlb_ref = next(it), next(it), next(it)
    kk_ref, ka_ref = next(it), next(it)
    r_o, lw_o, k_o, v_o, kk_o, b_o, z_o, sh_o = [next(it) for _ in range(8)]
    scr = next(it)

    tm = x_ref.shape[0]
    xn = _rms(x_ref[...], ln_ref[...])
    rolled = pltpu.roll(xn, 1, axis=0)
    row = lax.broadcasted_iota(jnp.int32, (tm, 1), 0)
    if per_row_shift:
        prev = jnp.where(row % seq_len == 0, sp_ref[...], rolled)
        for g in range(N_PAIRS):
            lanes = slice(g * LANES, (g + 1) * LANES)
            scr[g] = xn[:, lanes]
            sh_o[:, lanes] = scr[g, pl.ds(seq_len - 1, tm // seq_len, stride=seq_len), :]
    else:
        first = jnp.where(pl.program_id(1) == 0, sp_ref[...], scr[0:1, :])
        prev = jnp.where(row == 0, first, rolled)
        scr[0:1, :] = xn[tm - 1:tm, :]
        sh_o[...] = xn[tm - 1:tm, :]

    dx = prev - xn
    mu = mu_ref[...]

    def mixed(p):
        return xn + mu[p:p + 1, :] * dx

    xv = mixed(2)
    r = _dot(mixed(0).astype(BF16), win_ref[0])
    k = _dot(mixed(1).astype(BF16), win_ref[1])
    v = _dot(xv.astype(BF16), win_ref[2])
    z_o[...] = _dot(mixed(3).astype(BF16), win_ref[3])

    def lora(xm, la_ref, lb_ref, act):
        hdn = _dot(xm.astype(BF16), la_ref[...])
        if act:
            hdn = jnp.tanh(hdn)
        return _dot(hdn.astype(BF16), lb_ref[...])

    wl = w0_ref[...] + lora(mixed(4), wla_ref, wlb_ref, True)
    lw_o[...] = DECAY_LOG_SCALE * _sigmoid(wl)
    a = _sigmoid(a0_ref[...] + lora(mixed(5), ala_ref, alb_ref, False))
    if has_vres:
        v = v + (vf_ref[...] - v) * _sigmoid(v0_ref[...] + lora(xv, vla_ref, vlb_ref, False))

    bo = _block_ones()
    kkp = k * kk_ref[...]
    kk = kkp * lax.rsqrt(jnp.maximum(_head_sum(kkp * kkp, bo), 1e-24))
    r_o[...] = r
    k_o[...] = k * (1.0 + (a - 1.0) * ka_ref[...])
    v_o[...] = v
    kk_o[...] = kk
    b_o[...] = kk * a


def _a_pre(x, shift_prev, v_first, w, l, n_seq, seq_len):
    n = x.shape[0]
    has_vres = v_first is not None
    per_row_shift = seq_len < 256
    if per_row_shift:
        tm = n
        grid = (1, 1)
        sp = jnp.repeat(shift_prev, seq_len, axis=0)
        sp_spec = pl.BlockSpec((tm, D_MODEL), lambda b, i: (0, 0))
        sh_shape = jax.ShapeDtypeStruct((n_seq, D_MODEL), F32)
        sh_spec = pl.BlockSpec((n_seq, D_MODEL), lambda b, i: (0, 0))
        scr = pltpu.VMEM((N_PAIRS, tm, LANES), F32)
        tiles = 1
    else:
        tm = 256
        tiles = seq_len // tm
        grid = (n_seq, tiles)
        sp = shift_prev.reshape(n_seq, 1, D_MODEL)
        sp_spec = pl.BlockSpec((None, 1, D_MODEL), lambda b, i: (b, 0, 0))
        sh_shape = jax.ShapeDtypeStruct((n_seq, 1, D_MODEL), F32)
        sh_spec = pl.BlockSpec((None, 1, D_MODEL), lambda b, i: (b, 0, 0))
        scr = pltpu.VMEM((8, D_MODEL), F32)

    row_spec = pl.BlockSpec((tm, D_MODEL), lambda b, i: (b * tiles + i, 0))

    def of_layer(name, idx):
        arr = w[name]
        return arr, pl.BlockSpec((None,) + arr.shape[1:], lambda b, i: (idx,) + (0,) * (arr.ndim - 1))

    picks = [of_layer(name, l) for name in ('ln', 'mu', 'w_in', 'w0', 'w_la', 'w_lb', 'a0', 'a_la', 'a_lb')]
    if has_vres:
        picks += [of_layer(name, l - 1) for name in ('v0', 'v_la', 'v_lb')]
    picks += [of_layer(name, l) for name in ('k_k', 'k_a')]
    args = [x, sp] + ([v_first] if has_vres else []) + [a for a, _ in picks]
    specs = [row_spec, sp_spec] + ([row_spec] if has_vres else []) + [s for _, s in picks]

    full = jax.ShapeDtypeStruct((n, D_MODEL), F32)
    outs = pl.pallas_call(
        functools.partial(_a_pre_body, has_vres, per_row_shift, seq_len),
        out_shape=[full] * 7 + [sh_shape],
        grid=grid,
        in_specs=specs,
        out_specs=[row_spec] * 7 + [sh_spec],
        scratch_shapes=[scr],
        compiler_params=pltpu.CompilerParams(dimension_semantics=("arbitrary", "arbitrary"),
                                             vmem_limit_bytes=VMEM_LIMIT),
        name="a_pre",
    )(*args)
    return outs[:7], outs[7].reshape(n_seq, D_MODEL)


def _wkv_body(zero_init, L, *refs):
    it = iter(refs)
    r_ref, lw_ref, k_ref, v_ref, kk_ref, b_ref = [next(it) for _ in range(6)]
    st_ref = None if zero_init else next(it)
    y_ref, so_ref = next(it), next(it)
    sbd = next(it)

    n_b = r_ref.shape[0]
    c = pl.program_id(1)
    nc = pl.num_programs(1)
    half = HEAD_DIM
    chains = [(bi, p) for bi in range(n_b) for p in range(N_PAIRS)]

    @pl.when(c == 0)
    def _():
        if zero_init:
            sbd[...] = jnp.zeros(sbd.shape, F32)
        else:
            zeros = jnp.zeros((half, half), F32)
            for bi, p in chains:
                top = jnp.concatenate([st_ref[bi, 2 * p], zeros], axis=1)
                bot = jnp.concatenate([zeros, st_ref[bi, 2 * p + 1]], axis=1)
                sbd[bi, p] = jnp.concatenate([top, bot], axis=0)

    rr = lax.broadcasted_iota(jnp.int32, (L, L), 0)
    cc = lax.broadcasted_iota(jnp.int32, (L, L), 1)
    tri = (rr >= cc).astype(F32)
    lane = lax.broadcasted_iota(jnp.int32, (1, LANES), 1)
    h0 = lane < half
    r2 = lax.broadcasted_iota(jnp.int32, (2 * L, 2 * L), 0)
    c2 = lax.broadcasted_iota(jnp.int32, (2 * L, 2 * L), 1)
    strict = r2 > c2
    incl = r2 >= c2
    eye = (r2 == c2).astype(F32)
    blk = min(16, L)
    diag_blk = (r2 // blk) == (c2 // blk)

    def split(x):
        return jnp.concatenate([jnp.where(h0, x, 0.0), jnp.where(h0, 0.0, x)], axis=0).astype(BF16)

    def mm(a, b):
        return _dot(a.astype(BF16), b.astype(BF16))

    def neumann(a, n_terms):
        x = {ch: eye - a[ch] for ch in chains}
        pw, k = a, 2
        while k < n_terms:
            pw = {ch: mm(pw[ch], pw[ch]) for ch in chains}
            x = {ch: mm(x[ch], eye + pw[ch]) for ch in chains}
            k *= 2
        return x

    lhs4, bts, kls, v2s, bks, e_gls = {}, {}, {}, {}, {}, {}
    for bi in range(n_b):
        lw = lw_ref[bi]
        g_incl = jnp.dot(tri, lw, precision=_HI, preferred_element_type=F32)
        e_g = jnp.exp(g_incl)
        e_ng = jnp.exp(-g_incl)
        e_gl = e_g[L - 1:L, :]
        kt = kk_ref[bi] * jnp.exp(g_incl - lw)
        rt = r_ref[bi] * e_g
        bt = b_ref[bi] * e_ng
        kl = k_ref[bi] * e_ng
        bh = bt * e_gl
        kh = kl * e_gl
        vv = v_ref[bi]
        for p in range(N_PAIRS):
            sl = slice(p * LANES, (p + 1) * LANES)
            lhs4[bi, p] = jnp.concatenate([split(kt[:, sl]), split(rt[:, sl])], axis=0)
            bts[bi, p] = split(bt[:, sl])
            kls[bi, p] = split(kl[:, sl])
            v2s[bi, p] = split(vv[:, sl])
            bks[bi, p] = jnp.concatenate([split(bh[:, sl]), split(kh[:, sl])], axis=0)
            e_gls[bi, p] = e_gl[:, sl]

    wide_ok = (2 * L) % LANES == 0
    if wide_ok:
        aa = {ch: _dot_nt(lhs4[ch], jnp.concatenate([bts[ch], kls[ch]], axis=0)) for ch in chains}
        ab = {ch: aa[ch][:, 0:2 * L] for ch in chains}
        ak = {ch: aa[ch][:, 2 * L:4 * L] for ch in chains}
    else:
        ab = {ch: _dot_nt(lhs4[ch], bts[ch]) for ch in chains}
        ak = {ch: _dot_nt(lhs4[ch], kls[ch]) for ch in chains}
    a_kb = {ch: jnp.where(strict, ab[ch][0:2 * L], 0.0) for ch in chains}
    a_rb = {ch: jnp.where(incl, ab[ch][2 * L:4 * L], 0.0).astype(BF16) for ch in chains}
    a_kk = {ch: jnp.where(strict, ak[ch][0:2 * L], 0.0).astype(BF16) for ch in chains}
    a_rk = {ch: jnp.where(incl, ak[ch][2 * L:4 * L], 0.0).astype(BF16) for ch in chains}
    if L > blk:
        a_d = {ch: jnp.where(diag_blk, a_kb[ch], 0.0) for ch in chains}
        x_d = neumann(a_d, blk)
        nn = {ch: mm(x_d[ch], a_kb[ch] - a_d[ch]) for ch in chains}
        x_n = neumann(nn, L // blk)
        x_f = {ch: mm(x_n[ch], x_d[ch]).astype(BF16) for ch in chains}
    else:
        x_f = {ch: v.astype(BF16) for ch, v in neumann(a_kb, L).items()}

    s_old = {ch: sbd[ch[0], ch[1]] for ch in chains}
    pp = {ch: _dot_nt(lhs4[ch], s_old[ch].astype(BF16)) for ch in chains}
    rhs = {ch: -(pp[ch][0:2 * L] + _dot(a_kk[ch], v2s[ch])) for ch in chains}
    u2 = {ch: _dot(x_f[ch], rhs[ch].astype(BF16)).astype(BF16) for ch in chains}
    for ch in chains:
        bi, p = ch
        uv = jnp.concatenate([u2[ch], v2s[ch]], axis=0)
        if wide_ok:
            ys = pp[ch][2 * L:4 * L] + _dot(jnp.concatenate([a_rb[ch], a_rk[ch]], axis=1), uv)
        else:
            ys = pp[ch][2 * L:4 * L] + _dot(a_rb[ch], u2[ch]) + _dot(a_rk[ch], v2s[ch])
        y_ref[bi, :, p * LANES:(p + 1) * LANES] = ys[0:L] + ys[L:2 * L]
        sbd[bi, p] = s_old[ch] * e_gls[ch] + _dot_tn(uv, bks[ch])

    @pl.when(c == nc - 1)
    def _():
        for bi, p in chains:
            s_p = sbd[bi, p]
            so_ref[bi, 2 * p] = s_p[0:half, 0:half]
            so_ref[bi, 2 * p + 1] = pltpu.roll(s_p[half:2 * half, :], half, axis=1)[:, 0:half]


def _wkv(seqs, state0, chunk, rows_per_step):
    bsz, t, _ = seqs[0].shape
    zero_init = state0 is None
    nb = rows_per_step
    tok = pl.BlockSpec((nb, chunk, D_MODEL), lambda b, c: (b, c, 0))
    st = pl.BlockSpec((nb, N_HEADS, HEAD_DIM, HEAD_DIM), lambda b, c: (b, 0, 0, 0))
    args = list(seqs) + ([] if zero_init else [state0])
    specs = [tok] * 6 + ([] if zero_init else [st])
    return pl.pallas_call(
        functools.partial(_wkv_body, zero_init, chunk),
        out_shape=[jax.ShapeDtypeStruct((bsz, t, D_MODEL), F32),
                   jax.ShapeDtypeStruct((bsz, N_HEADS, HEAD_DIM, HEAD_DIM), F32)],
        grid=(bsz // nb, t // chunk),
        in_specs=specs,
        out_specs=[tok, st],
        scratch_shapes=[pltpu.VMEM((nb, N_PAIRS, LANES, LANES), F32)],
        compiler_params=pltpu.CompilerParams(dimension_semantics=("arbitrary", "arbitrary"),
                                             vmem_limit_bytes=VMEM_LIMIT),
        name="wkv",
    )(*args)


def _a_post_body(y_ref, r_ref, k_ref, v_ref, z_ref, x_ref, rk_ref, gw_ref, gb_ref, wo_ref, o_ref):
    bo = _block_ones()
    y = y_ref[...]
    inv_c = 1.0 / HEAD_DIM
    yc = y - _head_sum(y, bo) * inv_c
    var = _head_sum(yc * yc, bo) * inv_c
    yn = yc * lax.rsqrt(var + GN_EPS) * gw_ref[...] + gb_ref[...]
    yn = yn + _head_sum(r_ref[...] * k_ref[...] * rk_ref[...], bo) * v_ref[...]
    z = z_ref[...]
    gated = yn * (z * _sigmoid(z))
    o_ref[...] = x_ref[...] + _dot(gated.astype(BF16), wo_ref[...])


def _a_post(y, r, k, v, z, x, w, l):
    n = x.shape[0]
    tm = 512
    row = pl.BlockSpec((tm, D_MODEL), lambda i: (i, 0))
    names = ('r_k', 'gn_w', 'gn_b', 'w_out')
    picks = [pl.BlockSpec((None,) + w[nm].shape[1:], lambda i, nd=w[nm].ndim: (l,) + (0,) * (nd - 1)) for nm in names]
    return pl.pallas_call(
        _a_post_body,
        out_shape=jax.ShapeDtypeStruct((n, D_MODEL), F32),
        grid=(n // tm,),
        in_specs=[row] * 6 + picks,
        out_specs=row,
        compiler_params=pltpu.CompilerParams(dimension_semantics=("arbitrary",), vmem_limit_bytes=VMEM_LIMIT),
        name="a_post",
    )(y, r, k, v, z, x, *[w[nm] for nm in names])


def _rope_tile(t, cos_t, sin_t):
    lane = lax.broadcasted_iota(jnp.int32, (1, LANES), 1)
    first_half = (lane % HEAD_DIM) < (HEAD_DIM // 2)
    outs = []
    for g in range(t.shape[1] // LANES):
        xg = t[:, g * LANES:(g + 1) * LANES]
        partner = jnp.where(first_half, pltpu.roll(xg, LANES - HEAD_DIM // 2, axis=1),
                            pltpu.roll(xg, HEAD_DIM // 2, axis=1))
        outs.append(xg * cos_t + partner * sin_t)
    return jnp.concatenate(outs, axis=1)


def _proj_body(n_rope, pair_major, n_tails, x_ref, ln_ref, w_ref, cos_ref, sin_ref, o_ref, *tail_refs):
    xn = _rms(x_ref[...], ln_ref[...]).astype(BF16)
    for j in range(w_ref.shape[1] // D_MODEL):
        cols = slice(j * D_MODEL, (j + 1) * D_MODEL)
        t = _dot(xn, w_ref[:, cols])
        if j < n_rope:
            t = _rope_tile(t, cos_ref[...], sin_ref[...])
        if pair_major:
            for p in range(N_PAIRS):
                o_ref[j * N_PAIRS + p] = t[:, p * LANES:(p + 1) * LANES]
        else:
            o_ref[:, cols] = t
        if n_tails:
            tail = tail_refs[j % n_tails]
            keep = tail.shape[2]
            tail[j // n_tails] = t[t.shape[0] - keep:, :].T


def _proj(x, ln, w, cos_t, sin_t, n_rope, seqs=None, tails=None):
    n = x.shape[0]
    n_out = w.shape[1]
    tm = min(256, n)
    n_tab = cos_t.shape[0] // tm
    tab = pl.BlockSpec((tm, LANES), lambda i: (i % n_tab, 0))
    if seqs is None:
        out_shape = [jax.ShapeDtypeStruct((n, n_out), F32)]
        out_spec = [pl.BlockSpec((tm, n_out), lambda i: (i, 0))]
    else:
        bsz, seq_len = seqs
        tiles = seq_len // tm
        out_shape = [jax.ShapeDtypeStruct((bsz, n_out // LANES, seq_len, LANES), F32)]
        out_spec = [pl.BlockSpec((None, n_out // LANES, tm, LANES), lambda i: (i // tiles, 0, i % tiles, 0))]
    for rows in tails or ():
        width = min(rows, tm)
        first = tiles - rows // width
        out_shape.append(jax.ShapeDtypeStruct((bsz, 2, D_MODEL, rows), F32))
        out_spec.append(pl.BlockSpec((None, 2, D_MODEL, width),
                                     lambda i, first=first: (i // tiles, 0, 0, jnp.maximum(i % tiles - first, 0))))
    outs = pl.pallas_call(
        functools.partial(_proj_body, n_rope, seqs is not None, len(tails or ())),
        out_shape=out_shape,
        grid=(n // tm,),
        in_specs=[pl.BlockSpec((tm, D_MODEL), lambda i: (i, 0)),
                  pl.BlockSpec((1, D_MODEL), lambda i: (0, 0)),
                  pl.BlockSpec((D_MODEL, n_out), lambda i: (0, 0)),
                  tab, tab],
        out_specs=out_spec,
        compiler_params=pltpu.CompilerParams(dimension_semantics=("arbitrary",), vmem_limit_bytes=VMEM_LIMIT),
        name="proj",
    )(x, ln, w, cos_t, sin_t)
    return outs if tails else outs[0]


def _attn_prompt_body(seq_len, q0, q1, q2, k0, k1, k2, v0, v1, v2, o_ref, m_s, l_s, a_s):
    nq = KEYS_PER_QUERY
    lane = lax.broadcasted_iota(jnp.int32, (1, LANES), 1)
    h0 = lane < HEAD_DIM
    qi = lax.broadcasted_iota(jnp.int32, (nq, nq), 0)
    ki = lax.broadcasted_iota(jnp.int32, (nq, nq), 1)
    cur_mask = ki <= qi
    prev_mask = ki > qi

    heads = (h0, jnp.logical_not(h0))

    def run_blocks(g, q_ref, k_ref, v_ref, d, blocks):
        rows, qs, kbs, vbs, masks = [], [], [], [], []
        cur = []
        for c, nb, prev in blocks:
            rw = pl.ds(c + d * nq * nb, nq, stride=d)
            cur.append((k_ref[rw, :].astype(BF16), v_ref[rw, :].astype(BF16)))
            rows.append(rw)
        for i, (c, nb, prev) in enumerate(blocks):
            kb, vb = cur[i]
            mask = cur_mask
            if prev is not None:
                if prev is True and i > 0:
                    kp, vp = cur[i - 1]
                else:
                    pb = nb - 1 if prev is True else jnp.maximum(nb - 1, 0)
                    prow = pl.ds(c + d * nq * pb, nq, stride=d)
                    kp, vp = k_ref[prow, :].astype(BF16), v_ref[prow, :].astype(BF16)
                kb = jnp.concatenate([kp, kb], axis=0)
                vb = jnp.concatenate([vp, vb], axis=0)
                pm = prev_mask if prev is True else jnp.logical_and(prev_mask, prev)
                mask = jnp.concatenate([pm, cur_mask], axis=1)
            rw = rows[i]
            qs.append(q_ref[rw, :] * ATT_SCALE)
            kbs.append(kb)
            vbs.append(vb)
            masks.append(mask)
        n = len(blocks)
        pairs = [(i, h) for i in range(n) for h in range(2)]
        s = {(i, h): _dot_nt(jnp.where(heads[h], qs[i], 0.0).astype(BF16), kbs[i]) for i, h in pairs}
        s = {ih: jnp.where(masks[ih[0]], s[ih], NEG_INF) for ih in pairs}
        mx = {ih: jnp.max(s[ih], axis=-1, keepdims=True) for ih in pairs}
        pr = {ih: jnp.exp(s[ih] - mx[ih]) for ih in pairs}
        ls = {ih: jnp.sum(pr[ih], axis=-1, keepdims=True) for ih in pairs}
        acc = {(i, h): _dot(pr[i, h].astype(BF16), vbs[i]) for i, h in pairs}
        for i in range(n):
            m_b = jnp.where(h0, mx[i, 0], mx[i, 1])
            l_b = jnp.where(h0, ls[i, 0], ls[i, 1])
            a_b = jnp.where(h0, acc[i, 0], acc[i, 1])
            if g > 0:
                m_o, l_o, a_o = m_s[rows[i], :], l_s[rows[i], :], a_s[rows[i], :]
                m_n = jnp.maximum(m_o, m_b)
                w_o = jnp.exp(m_o - m_n)
                w_b = jnp.exp(m_b - m_n)
                m_b = m_n
                l_b = w_o * l_o + w_b * l_b
                a_b = w_o * a_o + w_b * a_b
            if g == N_GROUPS - 1:
                o_ref[rows[i], :] = a_b / l_b
            else:
                m_s[rows[i], :] = m_b
                l_s[rows[i], :] = l_b
                a_s[rows[i], :] = a_b

    u = ATTN_UNROLL
    for g, (q_ref, k_ref, v_ref) in enumerate(((q0, k0, v0), (q1, k1, v1), (q2, k2, v2))):
        d = DILATIONS[g]
        n_blk = seq_len // (d * nq)
        if n_blk == 1:
            def classes(i, carry, g=g, q_ref=q_ref, k_ref=k_ref, v_ref=v_ref, d=d):
                run_blocks(g, q_ref, k_ref, v_ref, d, [(i * u + j, 0, None) for j in range(u)])
                return carry
            lax.fori_loop(0, d // u, classes, 0)
        elif n_blk == u:
            def one_class(c, carry, g=g, q_ref=q_ref, k_ref=k_ref, v_ref=v_ref, d=d):
                run_blocks(g, q_ref, k_ref, v_ref, d, [(c, nb, True if nb else None) for nb in range(u)])
                return carry
            lax.fori_loop(0, d, one_class, 0)
        else:
            def blocks(i, carry, g=g, q_ref=q_ref, k_ref=k_ref, v_ref=v_ref, d=d):
                run_blocks(g, q_ref, k_ref, v_ref, d,
                           [(0, i * u + j, (i > 0) if j == 0 else True) for j in range(u)])
                return carry
            assert d == 1 and n_blk % u == 0
            lax.fori_loop(0, n_blk // u, blocks, 0)


def _attn_prompt(proj, kv):
    bsz, _, seq_len, _ = proj.shape

    def col(base):
        return pl.BlockSpec((None, None, seq_len, LANES), lambda b, p: (b, base + p, 0, 0))

    q_specs = [col(g * N_PAIRS) for g in range(N_GROUPS)]
    k_specs = [col(g * N_PAIRS) for g in range(N_GROUPS)]
    v_specs = [col((N_GROUPS + g) * N_PAIRS) for g in range(N_GROUPS)]
    return pl.pallas_call(
        functools.partial(_attn_prompt_body, seq_len),
        out_shape=jax.ShapeDtypeStruct((bsz, N_PAIRS, seq_len, LANES), F32),
        grid=(bsz, N_PAIRS),
        in_specs=q_specs + k_specs + v_specs,
        out_specs=col(0),
        scratch_shapes=[pltpu.VMEM((seq_len, LANES), F32)] * 3,
        compiler_params=pltpu.CompilerParams(dimension_semantics=("arbitrary", "arbitrary"),
                                             vmem_limit_bytes=VMEM_LIMIT),
        name="attn_prompt",
    )(proj, proj, proj, kv, kv, kv, kv, kv, kv)


def _attn_sample_body(n_new, n_split, compacted, emit, *refs):
    if emit:
        q_ref, kvn_ref, c0_ref, c1_ref, c2_ref, o_ref, cc_ref, m_s, l_s, acc_s = refs
    else:
        q_ref, kvn_ref, c0_ref, c1_ref, c2_ref, o_ref, m_s, l_s, acc_s = refs
    w = pl.program_id(1)
    rows = n_new * N_HEADS
    rid = lax.broadcasted_iota(jnp.int32, (rows, D_MODEL), 0)
    cid = lax.broadcasted_iota(jnp.int32, (rows, D_MODEL), 1)
    own_head = (rid % N_HEADS) == (cid // HEAD_DIM)
    jrow = lax.broadcasted_iota(jnp.int32, (rows, 1), 0) // N_HEADS

    def q_rows(g):
        qg = q_ref[:, g * D_MODEL:(g + 1) * D_MODEL] * ATT_SCALE
        full = jnp.concatenate([jnp.broadcast_to(qg[j:j + 1], (N_HEADS, D_MODEL)) for j in range(n_new)], axis=0)
        return jnp.where(own_head, full, 0.0)

    def absorb_cache(state, d, qb, kt, vt, first_row):
        s = _dot(qb.astype(BF16), kt)
        r = first_row + lax.broadcasted_iota(jnp.int32, (1, s.shape[1]), 1)
        s = jnp.where((r % d == jrow % d) & (r >= jrow + d), s, NEG_INF)
        m_c = jnp.max(s, axis=-1, keepdims=True)
        if state is None:
            m_n, alpha = m_c, None
        else:
            m_n = jnp.maximum(state[0], m_c)
            alpha = jnp.exp(state[0] - m_n)
        e = jnp.exp(s - m_n)
        l_c = jnp.sum(e, axis=-1, keepdims=True)
        a_c = _dot_nt(e.astype(BF16), vt)
        if state is None:
            return m_n, l_c, a_c
        return m_n, alpha * state[1] + l_c, alpha * state[2] + a_c

    def absorb_new(state, g, qb):
        d = DILATIONS[g]
        k_new = kvn_ref[:, g * D_MODEL:(g + 1) * D_MODEL]
        v_new = kvn_ref[:, (N_GROUPS + g) * D_MODEL:(N_GROUPS + g + 1) * D_MODEL]
        s = []
        for i in range(n_new):
            ok = (jrow >= i) & ((jrow - i) % d == 0)
            s.append(jnp.where(ok, jnp.sum(qb * k_new[i:i + 1], axis=-1, keepdims=True), NEG_INF))
        m_n = functools.reduce(jnp.maximum, s, state[0])
        alpha = jnp.exp(state[0] - m_n)
        l_n, a_n = alpha * state[1], alpha * state[2]
        for i in range(n_new):
            e = jnp.exp(s[i] - m_n)
            l_n = l_n + e
            a_n = a_n + e * v_new[i:i + 1]
        return m_n, l_n, a_n

    @pl.when(w == 0)
    def _():
        st = None
        for g, c_ref in ((0, c0_ref), (1, c1_ref)):
            qb = q_rows(g)
            st = absorb_cache(st, DILATIONS[g], qb, c_ref[0].astype(BF16), c_ref[1].astype(BF16), 0)
            st = absorb_new(st, g, qb)
        m_s[...], l_s[...], acc_s[...] = absorb_new(st, N_GROUPS - 1, q_rows(N_GROUPS - 1))

    wide = c2_ref.shape[2]
    kt, vt = c2_ref[0].astype(BF16), c2_ref[1].astype(BF16)
    lane_dil = COMPACT_DIL if compacted else DILATIONS[N_GROUPS - 1]
    m_s[...], l_s[...], acc_s[...] = absorb_cache((m_s[...], l_s[...], acc_s[...]), lane_dil,
                                                  q_rows(N_GROUPS - 1), kt, vt, w * wide)
    if emit:
        slab = LANES * COMPACT_DIL
        rr = lax.broadcasted_iota(jnp.int32, (slab, LANES), 0)
        cc = lax.broadcasted_iota(jnp.int32, (slab, LANES), 1)
        sel = (rr == DILATIONS[N_GROUPS - 1] * (cc // COMPACT_DIL) + cc % COMPACT_DIL).astype(BF16)
        for i, x in enumerate((kt, vt)):
            for k in range(wide // slab):
                cc_ref[i, :, k * LANES:(k + 1) * LANES] = _dot(x[:, k * slab:(k + 1) * slab], sel).astype(BF16)

    @pl.when(w == n_split - 1)
    def _():
        o = jnp.where(own_head, acc_s[...] / l_s[...], 0.0)
        o_ref[...] = jnp.sum(o.reshape(n_new, N_HEADS, D_MODEL), axis=1)


def _attn_sample(proj, kv, caches_t, bsz, n_new, emit):
    proj3 = proj.reshape(bsz, n_new, proj.shape[1])
    kv3 = kv.reshape(bsz, n_new, kv.shape[1])
    last = N_GROUPS - 1
    w_last = caches_t[last].shape[3]
    compacted = w_last != WINDOWS[last]
    assert n_new <= COMPACT_DIL and not (compacted and emit)
    assert w_last == (WINDOWS[last] // DILATIONS[last] * COMPACT_DIL if compacted else WINDOWS[last])
    n_split = 1 if compacted else ATTN_S_SPLIT
    specs = [pl.BlockSpec((None, 2, D_MODEL, WINDOWS[g]), lambda b, w: (b, 0, 0, 0)) for g in range(last)]
    specs.append(pl.BlockSpec((None, 2, D_MODEL, w_last // n_split), lambda b, w: (b, 0, 0, w)))
    for g in range(last):
        assert caches_t[g].shape[3] == WINDOWS[g]
    rows = n_new * N_HEADS
    out_shape = [jax.ShapeDtypeStruct((bsz, n_new, D_MODEL), F32)]
    out_specs = [pl.BlockSpec((None, n_new, D_MODEL), lambda b, w: (b, 0, 0))]
    if emit:
        w_c = w_last // DILATIONS[last] * COMPACT_DIL
        out_shape.append(jax.ShapeDtypeStruct((bsz, 2, D_MODEL, w_c), BF16))
        out_specs.append(pl.BlockSpec((None, 2, D_MODEL, w_c // n_split), lambda b, w: (b, 0, 0, w)))
    outs = pl.pallas_call(
        functools.partial(_attn_sample_body, n_new, n_split, compacted, emit),
        out_shape=out_shape,
        grid=(bsz, n_split),
        in_specs=[pl.BlockSpec((None, n_new, proj.shape[1]), lambda b, w: (b, 0, 0)),
                  pl.BlockSpec((None, n_new, kv.shape[1]), lambda b, w: (b, 0, 0))] + specs,
        out_specs=out_specs,
        scratch_shapes=[pltpu.VMEM((rows, 1), F32), pltpu.VMEM((rows, 1), F32), pltpu.VMEM((rows, D_MODEL), F32)],
        compiler_params=pltpu.CompilerParams(dimension_semantics=("arbitrary", "arbitrary"),
                                             vmem_limit_bytes=VMEM_LIMIT),
        name="attn_sample",
    )(proj3, kv3, *caches_t)
    o = outs[0].reshape(bsz * n_new, D_MODEL)
    return (o, outs[1]) if emit else o


def _wide(ref):
    if len(ref.shape) == 2:
        return ref[...]
    return jnp.concatenate([ref[p] for p in range(ref.shape[0])], axis=1)


def _b_post_body(final, *refs):
    if final:
        o_ref, z_ref, x_ref, wo_ref, fl_ref, out_ref = refs
    else:
        o_ref, z_ref, x_ref, wo_ref, out_ref = refs
    z = _wide(z_ref)
    gated = _wide(o_ref) * (z * _sigmoid(z))
    h = x_ref[...] + _dot(gated.astype(BF16), wo_ref[...])
    out_ref[...] = _rms(h, fl_ref[...]) if final else h


def _b_post(o, proj, x, w_out, final_ln):
    n = x.shape[0]
    tm = min(512, n)
    final = final_ln is not None
    row = pl.BlockSpec((tm, D_MODEL), lambda i: (i, 0))
    if o.ndim == 2:
        o_spec, z_spec = row, pl.BlockSpec((tm, D_MODEL), lambda i: (i, N_GROUPS))
    else:
        tiles = o.shape[2] // tm
        o_spec = pl.BlockSpec((None, N_PAIRS, tm, LANES), lambda i: (i // tiles, 0, i % tiles, 0))
        z_spec = pl.BlockSpec((None, N_PAIRS, tm, LANES), lambda i: (i // tiles, N_GROUPS, i % tiles, 0))
    specs = [o_spec, z_spec, row, pl.BlockSpec((D_MODEL, D_MODEL), lambda i: (0, 0))]
    args = [o, proj, x, w_out]
    if final:
        specs.append(pl.BlockSpec((1, D_MODEL), lambda i: (0, 0)))
        args.append(final_ln)
    return pl.pallas_call(
        functools.partial(_b_post_body, final),
        out_shape=jax.ShapeDtypeStruct((n, D_MODEL), F32),
        grid=(n // tm,),
        in_specs=specs,
        out_specs=row,
        compiler_params=pltpu.CompilerParams(dimension_semantics=("arbitrary",), vmem_limit_bytes=VMEM_LIMIT),
        name="b_post",
    )(*args)


def _rope_tables(pos):
    half = HEAD_DIM // 2
    inv = ROPE_THETA ** (-jnp.arange(half, dtype=F32) / half)
    ang = pos.astype(F32)[:, None] * inv[None, :]
    cos, sin = jnp.cos(ang), jnp.sin(ang)
    return jnp.concatenate([cos, cos, cos, cos], axis=1), jnp.concatenate([-sin, sin, -sin, sin], axis=1)


def _rwkv_stack(x, shift_init, wkv_init, n_seq, seq_len, w, n_layers):
    shifts, states = [], []
    v_first = None
    if seq_len % WKV_CHUNK == 0:
        chunk, rows_per_step, pad = WKV_CHUNK, 4, 0
    else:
        chunk, rows_per_step = WKV_SMALL_CHUNK, 8
        pad = (-seq_len) % chunk
    for l in range(n_layers):
        (r, lw, k, v, kk, b, z), sh = _a_pre(x, shift_init[l], v_first, w, l, n_seq, seq_len)
        if l == 0:
            v_first = v
        seqs = [t.reshape(n_seq, seq_len, D_MODEL) for t in (r, lw, k, v, kk, b)]
        if pad:
            seqs = [jnp.pad(t, ((0, 0), (0, pad), (0, 0))) for t in seqs]
        y, st = _wkv(seqs, None if wkv_init is None else wkv_init[l], chunk, rows_per_step)
        y = y[:, :seq_len].reshape(n_seq * seq_len, D_MODEL)
        x = _a_post(y, r, k, v, z, x, w, l)
        shifts.append(sh)
        states.append(st)
    return x, jnp.stack(shifts), jnp.stack(states)


def kernel(x_prompt, x_sample, state_wkv, state_shift, cache_kv_g0, cache_kv_g1, cache_kv_g2, a_ln, a_mu, a_w_in, a_w0, a_w_lora_a, a_w_lora_b, a_a0, a_a_lora_a, a_a_lora_b, a_v0, a_v_lora_a, a_v_lora_b, a_k_k, a_k_a, a_r_k, a_gn_w, a_gn_b, a_w_out, kv_ln, w_kv, b_ln, b_w_in, b_w_out, final_ln):
    bp, s, _ = x_prompt.shape
    bs, t, _ = x_sample.shape
    n_a = a_ln.shape[0]
    n_b = b_ln.shape[0]

    def row(vec):
        return vec.reshape(1, D_MODEL)

    def rows(stacked):
        return stacked.reshape(stacked.shape[0], 1, D_MODEL)

    w_a = dict(ln=rows(a_ln), mu=a_mu, w_in=a_w_in.astype(BF16), w0=rows(a_w0),
               w_la=a_w_lora_a.astype(BF16), w_lb=a_w_lora_b.astype(BF16), a0=rows(a_a0),
               a_la=a_a_lora_a.astype(BF16), a_lb=a_a_lora_b.astype(BF16),
               v0=rows(a_v0), v_la=a_v_lora_a.astype(BF16), v_lb=a_v_lora_b.astype(BF16),
               k_k=rows(a_k_k), k_a=rows(a_k_a), r_k=rows(a_r_k), gn_w=rows(a_gn_w),
               gn_b=rows(a_gn_b), w_out=a_w_out.astype(BF16))

    xp = x_prompt.reshape(bp * s, D_MODEL)
    xs = x_sample.reshape(bs * t, D_MODEL)
    zero_shift = jnp.zeros((n_a, bp, D_MODEL), F32)
    hp, shift_p, wkv_p = _rwkv_stack(xp, zero_shift, None, bp, s, w_a, n_a)
    hs, shift_s, wkv_s = _rwkv_stack(xs, state_shift, state_wkv, bs, t, w_a, n_a)

    cos_p, sin_p = _rope_tables(jnp.arange(s, dtype=jnp.int32))
    cos_s, sin_s = _rope_tables(jnp.tile(PAST_LEN + jnp.arange(t, dtype=jnp.int32), bs))
    w_kv_b = w_kv.astype(BF16)
    kv_ln_r = row(kv_ln)
    tail_rows = [min(WINDOWS[g], s) for g in range(N_GROUPS)]
    kvp, *kv_tails = _proj(hp, kv_ln_r, w_kv_b, cos_p, sin_p, N_GROUPS, (bp, s), tail_rows)
    kvs = _proj(hs, kv_ln_r, w_kv_b, cos_s, sin_s, N_GROUPS)

    caches = tuple(jnp.transpose(c, (0, 2, 3, 4, 1)).reshape(bs, 2, D_MODEL, c.shape[1])
                   for c in (cache_kv_g0, cache_kv_g1, cache_kv_g2))
    for l in range(n_b):
        w_in = b_w_in[l].astype(BF16)
        w_out = b_w_out[l].astype(BF16)
        fl = row(final_ln) if l == n_b - 1 else None
        pj = _proj(hp, row(b_ln[l]), w_in, cos_p, sin_p, N_GROUPS, (bp, s))
        hp = _b_post(_attn_prompt(pj, kvp), pj, hp, w_out, fl)
        pj = _proj(hs, row(b_ln[l]), w_in, cos_s, sin_s, N_GROUPS)
        if l == 0 and n_b > 1:
            o_s, compact = _attn_sample(pj, kvs, caches, bs, t, True)
            caches = caches[:-1] + (compact,)
        else:
            o_s = _attn_sample(pj, kvs, caches, bs, t, False)
        hs = _b_post(o_s, pj, hs, w_out, fl)

    y_prompt = hp.reshape(bp, s, D_MODEL)
    y_sample = hs.reshape(bs, t, D_MODEL)

    kvs3 = kvs.reshape(bs, t, 2, N_GROUPS, N_HEADS, HEAD_DIM)
    kv_out = []
    for g in range(N_GROUPS):
        rows_minor = kv_tails[g].reshape(bp, 2, N_HEADS, HEAD_DIM, tail_rows[g])
        kv_out.append(jnp.transpose(rows_minor, (0, 4, 1, 2, 3)))
        kv_out.append(kvs3[:, :, :, g])
    return (y_prompt, y_sample, wkv_p, wkv_s, shift_p, shift_s, *kv_out)
```
